```python
import jax
import jax.numpy as jnp
from jax import lax
import numpy as np

D_MODEL = 1024
BATCH = 4
SEQ = 4096
DEPTH = 4
DEC_BATCH = 128
DEC_SEQ = 4
PAST_LEN = 2048
PAGE_SIZE = 128

N_A = DEPTH // 2
N_B = DEPTH - N_A
RWKV_HEAD = 64
RWKV_H = D_MODEL // RWKV_HEAD
LORA_DECAY = 64
LORA_A = 64
LORA_V = 32
LORA_G = 160
GN_EPS = 64e-5
N_HEADS = 16
HEAD_DIM = 64
N_KV = 4
GROUP = N_HEADS // N_KV
CMP_STRIDE = 16
CMP_LEN = 2 * CMP_STRIDE
CMP_HID = 2 * HEAD_DIM
SEL_BLOCK = 64
N_SEL = 16
WINDOW = 512
Q_BLOCK = 64
QROW_BUDGET = 256
D_FF = 2816
RMS_EPS = 1e-6
NEG_INF = -1e30
FORCED_SCORE = 1e4

kernel_name = 'yoco_rwkv7_nsa_macaron_step'

RWKV_PARAMS = ('rw_mu', 'rw_w_rkv', 'rw_w0', 'rw_w1', 'rw_w2', 'rw_a0', 'rw_a1', 'rw_a2',
               'rw_g1', 'rw_g2', 'rw_k_k', 'rw_k_a', 'rw_r_k', 'rw_lnx_w', 'rw_lnx_b', 'rw_w_o')


def rms_norm(x, g):
    xf = x.astype(jnp.float32)
    y = xf * lax.rsqrt(jnp.mean(xf * xf, axis=-1, keepdims=True) + RMS_EPS)
    return (y * g.astype(jnp.float32)).astype(x.dtype)


def swiglu(x, w_in, w_out):
    gate, up = jnp.split(x @ w_in, 2, axis=-1)
    return (jax.nn.silu(gate) * up) @ w_out


def alibi_slopes(n):
    return jnp.exp2(-8.0 * jnp.arange(1, n + 1, dtype=jnp.float32) / n)


def query_block(n_rows, tq):
    return max(1, min(Q_BLOCK, tq, QROW_BUDGET // n_rows))


def rwkv7_mix(x, shift0, s0, v_first, ap, vres):
    (mu, w_rkv, w0, w1, w2, a0, a1, a2, g1, g2, k_k, k_a, r_k, lnx_w, lnx_b, w_o) = ap
    B, T, D = x.shape
    x_prev = jnp.concatenate([shift0[:, None, :].astype(x.dtype), x[:, :-1]], axis=1)
    xx = x_prev - x
    xs = x[None] + xx[None] * mu[:, None, None, :]
    r, k, v = jnp.einsum('cbtd,cde->cbte', xs[:3], w_rkv)
    xv, xw, xa, xg = xs[2], xs[3], xs[4], xs[5]
    w = -jax.nn.softplus(-(w0 + jnp.tanh(xw @ w1) @ w2)) - 0.5
    a = jax.nn.sigmoid(a0 + (xa @ a1) @ a2)
    g = jax.nn.sigmoid(xg @ g1) @ g2
    if vres is None:
        v_first = v
    else:
        v0, v1, v2 = vres
        v = v + (v_first - v) * jax.nn.sigmoid(v0 + (xv @ v1) @ v2)

    def heads(t):
        return t.astype(jnp.float32).reshape(B, T, RWKV_H, RWKV_HEAD)

    kk = heads(k * k_k)
    kk = kk * lax.rsqrt(jnp.maximum(jnp.sum(kk * kk, axis=-1, keepdims=True), 1e-24))
    k = k * (1.0 + (a - 1.0) * k_a)
    rh, kh, vh, ah = heads(r), heads(k), heads(v), heads(a)
    decay = jnp.exp(-jnp.exp(heads(w)))

    def tm(t):
        return jnp.swapaxes(t, 0, 1)

    def step(S, inp):
        r_t, w_t, k_t, v_t, a_t, b_t = inp
        sa = jnp.einsum('bhij,bhj->bhi', S, a_t)
        S = S * w_t[:, :, None, :] + sa[..., None] * b_t[:, :, None, :] + v_t[..., None] * k_t[:, :, None, :]
        return S, jnp.einsum('bhij,bhj->bhi', S, r_t)

    S_T, ys = lax.scan(step, s0.astype(jnp.float32),
                       (tm(rh), tm(decay), tm(kh), tm(vh), tm(-kk), tm(kk * ah)))
    y = jnp.swapaxes(ys, 0, 1)
    mean = jnp.mean(y, axis=-1, keepdims=True)
    var = jnp.mean(jnp.square(y - mean), axis=-1, keepdims=True)
    y = ((y - mean) * lax.rsqrt(var + GN_EPS)).reshape(B, T, D) * lnx_w + lnx_b
    bonus = jnp.sum(rh * kh * r_k.astype(jnp.float32), axis=-1, keepdims=True) * vh
    y = y + bonus.reshape(B, T, D)
    out = (y.astype(x.dtype) * g) @ w_o
    return out, x[:, -1], S_T.astype(s0.dtype), v_first


def nsa_compress(k, pos, w1, w2):
    B, tkp = k.shape[:2]
    n = tkp // CMP_STRIDE
    half = CMP_STRIDE * HEAD_DIM
    ch = k.reshape(B, n, CMP_STRIDE, N_KV, HEAD_DIM).transpose(0, 1, 3, 2, 4).reshape(B, n, N_KV, half)
    w1 = w1.astype(jnp.float32)
    h_lo = ch @ w1[:half]
    h_hi = ch @ w1[half:]
    h_hi = jnp.concatenate([h_hi[:, 1:], jnp.zeros_like(h_hi[:, :1])], axis=1)
    h = jax.nn.gelu(h_lo + h_hi + pos.astype(jnp.float32).reshape(-1) @ w1)
    return h @ w2.astype(jnp.float32)


def nsa_prepare(kc, vc, ks, vs, kw, vw, tq, cmp_pos, cmp_w1, cmp_w2):
    B, tk_pad = kc.shape[:2]
    k_cmp = nsa_compress(kc, cmp_pos[0], cmp_w1[0], cmp_w2[0]).transpose(0, 2, 1, 3)
    v_cmp = nsa_compress(vc, cmp_pos[1], cmp_w1[1], cmp_w2[1]).transpose(0, 2, 1, 3)
    n_sel = tk_pad // SEL_BLOCK

    def blocks(t):
        return t.reshape(B, n_sel, SEL_BLOCK, N_KV, HEAD_DIM).transpose(0, 3, 1, 2, 4)

    qb = query_block(B, tq)
    tq_pad = -(-tq // qb) * qb

    def pad_w(t):
        return jnp.pad(t, ((0, 0), (WINDOW, tq_pad - tq), (0, 0), (0, 0)))

    return (k_cmp, v_cmp, blocks(ks), blocks(vs), pad_w(kw), pad_w(vw))


def nsa_attend(q, gates, shared, q_pos0, w_pos0):
    k_cmp, v_cmp, ksb, vsb, kwp, vwp = shared
    B, TQ = q.shape[:2]
    NC, NS = k_cmp.shape[2], ksb.shape[2]
    n_top = min(N_SEL, NS)
    qb = query_block(B, TQ)
    n_qb = -(-TQ // qb)
    TQp = n_qb * qb
    padq = ((0, 0), (0, TQp - TQ), (0, 0), (0, 0))
    qg = jnp.pad(q.astype(jnp.float32) * HEAD_DIM ** -0.5, padq)
    qg = qg.reshape(B, n_qb, qb, N_KV, GROUP, HEAD_DIM).transpose(1, 0, 3, 4, 2, 5)
    gg = jnp.pad(gates, padq).reshape(B, n_qb, qb, N_KV, GROUP, 3).transpose(1, 0, 3, 4, 2, 5)
    slopes = alibi_slopes(N_HEADS).reshape(N_KV, GROUP)[None, :, :, None, None]
    cmp_start = jnp.arange(NC) * CMP_STRIDE
    cmp_end = cmp_start + (CMP_LEN - 1)
    blk = jnp.arange(NS)
    overlap = ((cmp_start[:, None] < (blk[None, :] + 1) * SEL_BLOCK)
               & (cmp_start[:, None] + CMP_LEN > blk[None, :] * SEL_BLOCK)).astype(jnp.float32)
    bi = jnp.arange(B)[:, None, None]
    hi = jnp.arange(N_KV)[None, :, None]
    within = jnp.arange(SEL_BLOCK)
    win_off = jnp.arange(WINDOW + qb)

    def one_block(args):
        i, qi, gi = args
        t0 = q_pos0 + i * qb
        t = t0 + jnp.arange(qb)
        d_c = t[:, None] - cmp_end[None, :]
        ok_c = d_c >= 0
        s = jnp.einsum('bkgqd,bknd->bkgqn', qi, k_cmp) - slopes * d_c.astype(jnp.float32)
        p_c = jnp.where(ok_c, jax.nn.softmax(jnp.where(ok_c, s, NEG_INF), axis=-1), 0.0)
        o_c = jnp.einsum('bkgqn,bknd->bkgqd', p_c, v_cmp)
        cur = t // SEL_BLOCK
        imp = jnp.einsum('bkgqn,ns->bkqs', p_c, overlap)
        forced = (blk[None, :] == 0) | (blk[None, :] == cur[:, None]) | (blk[None, :] == cur[:, None] - 1)
        imp = jnp.where(blk[None, :] <= cur[:, None], jnp.where(forced, FORCED_SCORE, imp), -1.0)
        _, idx = lax.top_k(imp, n_top)
        flat = idx.reshape(B, N_KV, qb * n_top)
        kg = ksb[bi, hi, flat].reshape(B, N_KV, qb, n_top * SEL_BLOCK, HEAD_DIM)
        vg = vsb[bi, hi, flat].reshape(B, N_KV, qb, n_top * SEL_BLOCK, HEAD_DIM)
        spos = (idx[..., None] * SEL_BLOCK + within).reshape(B, N_KV, qb, n_top * SEL_BLOCK)
        d_s = (t[:, None] - spos)[:, :, None]
        s = jnp.einsum('bkgqd,bkqsd->bkgqs', qi, kg) - slopes * d_s.astype(jnp.float32)
        p_s = jax.nn.softmax(jnp.where(d_s >= 0, s, NEG_INF), axis=-1)
        o_s = jnp.einsum('bkgqs,bkqsd->bkgqd', p_s, vg)
        start = t0 - w_pos0
        kw = lax.dynamic_slice_in_dim(kwp, start, WINDOW + qb, axis=1)
        vw = lax.dynamic_slice_in_dim(vwp, start, WINDOW + qb, axis=1)
        wpos = t0 - WINDOW + win_off
        d_w = t[:, None] - wpos[None, :]
        ok_w = (d_w >= 0) & (d_w <= WINDOW) & (wpos[None, :] >= w_pos0)
        s = jnp.einsum('bkgqd,bskd->bkgqs', qi, kw) - slopes * d_w.astype(jnp.float32)
        p_w = jax.nn.softmax(jnp.where(ok_w, s, NEG_INF), axis=-1)
        o_w = jnp.einsum('bkgqs,bskd->bkgqd', p_w, vw)
        return gi[..., 0:1] * o_c + gi[..., 1:2] * o_s + gi[..., 2:3] * o_w

    o = lax.map(one_block, (jnp.arange(n_qb), qg, gg))
    return o.transpose(1, 0, 4, 2, 3, 5).reshape(B, TQp, N_HEADS * HEAD_DIM)[:, :TQ]


def nsa_layer(xn, w_in, w_o, shared, q_pos0, w_pos0):
    B, T, _ = xn.shape
    nq = N_HEADS * HEAD_DIM
    z = xn @ w_in
    q = z[..., :nq].reshape(B, T, N_HEADS, HEAD_DIM)
    gates = jax.nn.sigmoid(z[..., nq:].astype(jnp.float32)).reshape(B, T, N_HEADS, 3)
    o = nsa_attend(q, gates, shared, q_pos0, w_pos0)
    return o.astype(xn.dtype) @ w_o


def trunk(x, shift_in, wkv_in, past_cs, win_buf, win_keep, q_pos0, P):
    B, T, _ = x.shape
    h = x
    shifts, states = [], []
    v_first = None
    shared = None
    w_pos0 = 0
    cs_new = None
    win_state = None
    for l in range(DEPTH):
        g = P['norm_g'][l]
        h = h + 0.5 * swiglu(rms_norm(h, g[0]), P['ffn_w_in'][l, 0], P['ffn_w_out'][l, 0])
        xn = rms_norm(h, g[1])
        if l < N_A:
            ap = tuple(P[n][l] for n in RWKV_PARAMS)
            vres = None if l == 0 else (P['rw_v0'][l - 1], P['rw_v1'][l - 1], P['rw_v2'][l - 1])
            mix, sh, st, v_first = rwkv7_mix(xn, shift_in[l], wkv_in[l], v_first, ap, vres)
            shifts.append(sh)
            states.append(st)
        else:
            mix = nsa_layer(xn, P['b_w_in'][l - N_A], P['b_w_o'][l - N_A], shared, q_pos0, w_pos0)
        h = h + mix
        h = h + 0.5 * swiglu(rms_norm(h, g[2]), P['ffn_w_in'][l, 1], P['ffn_w_out'][l, 1])
        if l == N_A - 1:
            kv = (rms_norm(h, P['kv_norm_g']) @ P['kv_w']).reshape(B, T, 6, N_KV, HEAD_DIM)
            cs_new = kv[:, :, :4]
            win_new = kv[:, :, 4:]
            tk = q_pos0 + T
            tk_pad = -(-tk // SEL_BLOCK) * SEL_BLOCK
            zpad = jnp.zeros((B, tk_pad - tk, N_KV, HEAD_DIM), jnp.float32)
            rows = []
            for c in range(4):
                past = [] if past_cs is None else [past_cs[:, :, c].astype(jnp.float32)]
                rows.append(jnp.concatenate(past + [cs_new[:, :, c].astype(jnp.float32), zpad], axis=1))
            win_all = win_new if win_buf is None else jnp.concatenate(
                [win_buf.astype(win_new.dtype), win_new], axis=1)
            w_pos0 = tk - win_all.shape[1]
            win_state = win_all[:, win_all.shape[1] - win_keep:]
            wf = win_all.astype(jnp.float32)
            shared = nsa_prepare(rows[0], rows[1], rows[2], rows[3], wf[:, :, 0], wf[:, :, 1], T,
                                 P['cmp_pos'], P['cmp_w1'], P['cmp_w2'])
    y = rms_norm(h, P['final_g'])
    return y, jnp.stack(shifts), jnp.stack(states), cs_new, win_state


def setup_inputs(seed: int = 0) -> dict:
    key = jax.random.key(seed)
    keys = jax.random.split(key, 64)
    counter = iter(range(64))

    def nk():
        return keys[next(counter)]

    def nrm(shape, scale=1.0):
        return scale * jax.random.normal(nk(), shape, jnp.float32)

    D = D_MODEL
    n_pages = PAST_LEN // PAGE_SIZE
    n_used = DEC_BATCH * n_pages
    n_pool = (5 * n_used + 3) // 4
    perm = jax.random.permutation(nk(), n_pool)
    page_table = perm[:n_used].reshape(DEC_BATCH, n_pages).astype(jnp.int32)
    ramp = (jnp.arange(D, dtype=jnp.float32) / (D - 1)) ** 1.2
    n_q = N_HEADS * HEAD_DIM
    return {
        'x_prompt': nrm((BATCH, SEQ, D)),
        'x_sample': nrm((DEC_BATCH, DEC_SEQ, D)),
        'cache_cs': nrm((n_pool, PAGE_SIZE, 4, N_KV, HEAD_DIM)),
        'cache_win': nrm((DEC_BATCH, min(WINDOW, PAST_LEN), 2, N_KV, HEAD_DIM)),
        'state_shift': nrm((N_A, DEC_BATCH, D)),
        'state_wkv': nrm((N_A, DEC_BATCH, RWKV_H, RWKV_HEAD, RWKV_HEAD), 0.5),
        'page_table': page_table,
        'norm_g': 1.0 + nrm((DEPTH, 3, D), 0.02),
        'ffn_w_in': nrm((DEPTH, 2, D, 2 * D_FF), D ** -0.5),
        'ffn_w_out': nrm((DEPTH, 2, D_FF, D), D_FF ** -0.5),
        'rw_mu': jax.random.uniform(nk(), (N_A, 6, D), jnp.float32),
        'rw_w_rkv': nrm((N_A, 3, D, D), D ** -0.5),
        'rw_w0': -6.5 + 5.0 * ramp + nrm((N_A, D), 0.1),
        'rw_w1': nrm((N_A, D, LORA_DECAY), D ** -0.5),
        'rw_w2': nrm((N_A, LORA_DECAY, D), 0.05),
        'rw_a0': nrm((N_A, D), 0.1),
        'rw_a1': nrm((N_A, D, LORA_A), D ** -0.5),
        'rw_a2': nrm((N_A, LORA_A, D), 0.1 * LORA_A ** -0.5),
        'rw_g1': nrm((N_A, D, LORA_G), D ** -0.5),
        'rw_g2': nrm((N_A, LORA_G, D), LORA_G ** -0.5),
        'rw_k_k': 0.85 + nrm((N_A, D), 0.05),
        'rw_k_a': 1.0 + nrm((N_A, D), 0.05),
        'rw_r_k': nrm((N_A, RWKV_H, RWKV_HEAD), 0.1),
        'rw_lnx_w': 1.0 + nrm((N_A, D), 0.02),
        'rw_lnx_b': nrm((N_A, D), 0.02),
        'rw_w_o': nrm((N_A, D, D), D ** -0.5),
        'rw_v0': 1.0 + nrm((N_A - 1, D), 0.1),
        'rw_v1': nrm((N_A - 1, D, LORA_V), D ** -0.5),
        'rw_v2': nrm((N_A - 1, LORA_V, D), 0.1 * LORA_V ** -0.5),
        'kv_norm_g': 1.0 + nrm((D,), 0.02),
        'kv_w': nrm((D, 6 * N_KV * HEAD_DIM), D ** -0.5),
        'cmp_pos': nrm((2, CMP_LEN, HEAD_DIM), 0.1),
        'cmp_w1': nrm((2, CMP_LEN * HEAD_DIM, CMP_HID), (CMP_LEN * HEAD_DIM) ** -0.5),
        'cmp_w2': nrm((2, CMP_HID, HEAD_DIM), CMP_HID ** -0.5),
        'b_w_in': nrm((N_B, D, n_q + 3 * N_HEADS), D ** -0.5),
        'b_w_o': nrm((N_B, n_q, D), n_q ** -0.5),
        'final_g': 1.0 + nrm((D,), 0.02),
    }


def reference(x_prompt, x_sample, cache_cs, cache_win, state_shift, state_wkv, page_table,
              norm_g, ffn_w_in, ffn_w_out, rw_mu, rw_w_rkv, rw_w0, rw_w1, rw_w2, rw_a0, rw_a1,
              rw_a2, rw_g1, rw_g2, rw_k_k, rw_k_a, rw_r_k, rw_lnx_w, rw_lnx_b, rw_w_o, rw_v0,
              rw_v1, rw_v2, kv_norm_g, kv_w, cmp_pos, cmp_w1, cmp_w2, b_w_in, b_w_o, final_g):
    P = dict(norm_g=norm_g, ffn_w_in=ffn_w_in, ffn_w_out=ffn_w_out, rw_mu=rw_mu,
             rw_w_rkv=rw_w_rkv, rw_w0=rw_w0, rw_w1=rw_w1, rw_w2=rw_w2, rw_a0=rw_a0,
             rw_a1=rw_a1, rw_a2=rw_a2, rw_g1=rw_g1, rw_g2=rw_g2, rw_k_k=rw_k_k, rw_k_a=rw_k_a,
             rw_r_k=rw_r_k, rw_lnx_w=rw_lnx_w, rw_lnx_b=rw_lnx_b, rw_w_o=rw_w_o, rw_v0=rw_v0,
             rw_v1=rw_v1, rw_v2=rw_v2, kv_norm_g=kv_norm_g, kv_w=kv_w, cmp_pos=cmp_pos,
             cmp_w1=cmp_w1, cmp_w2=cmp_w2, b_w_in=b_w_in, b_w_o=b_w_o, final_g=final_g)
    bp, tp = x_prompt.shape[:2]
    zero_shift = jnp.zeros((N_A, bp, D_MODEL), x_prompt.dtype)
    zero_wkv = jnp.zeros((N_A, bp, RWKV_H, RWKV_HEAD, RWKV_HEAD), jnp.float32)
    y_prompt, p_shift, p_wkv, p_cs, p_win = trunk(
        x_prompt, zero_shift, zero_wkv, None, None, min(WINDOW, tp), 0, P)
    n_seq, n_pages = page_table.shape
    past_len = n_pages * cache_cs.shape[1]
    past_cs = cache_cs[page_table].reshape(n_seq, past_len, 4, N_KV, HEAD_DIM)
    y_sample, s_shift, s_wkv, s_cs, s_win = trunk(
        x_sample, state_shift, state_wkv, past_cs, cache_win, cache_win.shape[1], past_len, P)
    return (y_prompt, y_sample, p_shift, p_wkv, p_cs, p_win, s_shift, s_wkv, s_cs, s_win)
```

```python
import functools

import jax
import jax.numpy as jnp
from jax import lax
from jax.experimental import pallas as pl
from jax.experimental.pallas import tpu as pltpu

F32 = jnp.float32
BF16 = jnp.bfloat16

RWKV_HEAD = 64
GN_EPS = 64e-5
N_HEADS = 16
HEAD_DIM = 64
N_KV = 4
GROUP = N_HEADS // N_KV
KV_W = N_KV * HEAD_DIM
CMP_STRIDE = 16
CMP_LEN = 2 * CMP_STRIDE
CMP_HID = 2 * HEAD_DIM
SEL_BLOCK = 64
N_SEL = 16
WINDOW = 512
RMS_EPS = 1e-6
NEG_INF = -1e30
FORCED_SCORE = 1e4
REMOVED = -3e38

SCAN_CHUNK = 64
SCAN_SUB = 16
KEY_TILE = 512
VMEM_LIMIT = 56 * 1024 * 1024


def _cparams(*sem):
    return pltpu.CompilerParams(dimension_semantics=sem, vmem_limit_bytes=VMEM_LIMIT)


def _row_tile(m, cap):
    t = cap
    while t > 8 and m % t:
        t //= 2
    assert m % t == 0, (m, cap)
    return t


def _rms(x, g):
    return x * lax.rsqrt(jnp.mean(x * x, axis=-1, keepdims=True) + RMS_EPS) * g


def _dot(a, b):
    return jnp.dot(a, b, preferred_element_type=F32)


def _dot_nt(a, b):
    return lax.dot_general(a, b, (((1,), (1,)), ((), ())), preferred_element_type=F32)


def _dot_tn(a, b):
    return lax.dot_general(a, b, (((0,), (0,)), ((), ())), preferred_element_type=F32)


def _ffn_body(nf, post, h_ref, g_ref, wg_ref, wu_ref, wo_ref, gp_ref, *rest):
    if post:
        o_ref, on_ref, xn_sc, acc_sc = rest
    else:
        o_ref, xn_sc, acc_sc = rest
    f = pl.program_id(1)

    @pl.when(f == 0)
    def _():
        xn_sc[...] = _rms(h_ref[...], g_ref[...]).astype(BF16)
        acc_sc[...] = jnp.zeros_like(acc_sc)

    xn = xn_sc[...]
    gate = _dot(xn, wg_ref[...])
    up = _dot(xn, wu_ref[...])
    hid = (gate * jax.nn.sigmoid(gate) * up).astype(BF16)
    acc_sc[...] += _dot(hid, wo_ref[...])

    @pl.when(f == nf - 1)
    def _():
        hn = h_ref[...] + 0.5 * acc_sc[...]
        o_ref[...] = hn
        if post:
            on_ref[...] = _rms(hn, gp_ref[...])


def _ffn(h, g, w_in, w_out, g_post=None):
    m, d = h.shape
    dff = w_out.shape[0]
    tm = _row_tile(m, 512)
    tf = dff // 2 if (dff // 2) % 128 == 0 else dff
    nf = dff // tf
    post = g_post is not None
    gp = g_post if post else g
    out_shape = [jax.ShapeDtypeStruct((m, d), F32)]
    out_specs = [pl.BlockSpec((tm, d), lambda i, f: (i, 0))]
    if post:
        out_shape.append(jax.ShapeDtypeStruct((m, d), F32))
        out_specs.append(pl.BlockSpec((tm, d), lambda i, f: (i, 0)))
    res = pl.pallas_call(
        functools.partial(_ffn_body, nf, post),
        grid=(m // tm, nf),
        in_specs=[
            pl.BlockSpec((tm, d), lambda i, f: (i, 0)),
            pl.BlockSpec((1, d), lambda i, f: (0, 0)),
            pl.BlockSpec((d, tf), lambda i, f: (0, f)),
            pl.BlockSpec((d, tf), lambda i, f: (0, nf + f)),
            pl.BlockSpec((tf, d), lambda i, f: (f, 0)),
            pl.BlockSpec((1, d), lambda i, f: (0, 0)),
        ],
        out_specs=out_specs,
        out_shape=out_shape,
        scratch_shapes=[pltpu.VMEM((tm, d), BF16), pltpu.VMEM((tm, d), F32)],
        compiler_params=_cparams("parallel", "arbitrary"),
        name="ffn",
    )(h, g.reshape(1, d), w_in, w_in, w_out, gp.reshape(1, d))
    return res if post else res[0]


def _norm_matmul_body(norm, x_ref, g_ref, w_ref, o_ref, ob_ref):
    x = x_ref[...]
    if norm:
        x = _rms(x, g_ref[...])
    y = _dot(x.astype(BF16), w_ref[...])
    o_ref[...] = y
    ob_ref[...] = y.astype(BF16)


def _norm_matmul(x, g, w):
    m, d = x.shape
    n = w.shape[1]
    tm = _row_tile(m, 512)
    norm = g is not None
    gg = g.reshape(1, d) if norm else jnp.ones((1, d), F32)
    return pl.pallas_call(
        functools.partial(_norm_matmul_body, norm),
        grid=(m // tm,),
        in_specs=[
            pl.BlockSpec((tm, d), lambda i: (i, 0)),
            pl.BlockSpec((1, d), lambda i: (0, 0)),
            pl.BlockSpec((d, n), lambda i: (0, 0)),
        ],
        out_specs=[pl.BlockSpec((tm, n), lambda i: (i, 0)),
                   pl.BlockSpec((tm, n), lambda i: (i, 0))],
        out_shape=[jax.ShapeDtypeStruct((m, n), F32), jax.ShapeDtypeStruct((m, n), BF16)],
        compiler_params=_cparams("parallel"),
        name="norm_matmul",
    )(x, gg, w)


def _matmul_res_body(gated, x_ref, g_ref, w_ref, h_ref, o_ref):
    x = x_ref[...]
    if gated:
        x = x * g_ref[...]
    o_ref[...] = h_ref[...] + _dot(x.astype(BF16), w_ref[...])


def _matmul_res(x, gate, w, h):
    m, k = x.shape
    n = w.shape[1]
    tm = _row_tile(m, 512)
    gated = gate is not None
    gg = gate if gated else x
    return pl.pallas_call(
        functools.partial(_matmul_res_body, gated),
        grid=(m // tm,),
        in_specs=[
            pl.BlockSpec((tm, k), lambda i: (i, 0)),
            pl.BlockSpec((tm, k), lambda i: (i, 0)),
            pl.BlockSpec((k, n), lambda i: (0, 0)),
            pl.BlockSpec((tm, n), lambda i: (i, 0)),
        ],
        out_specs=pl.BlockSpec((tm, n), lambda i: (i, 0)),
        out_shape=jax.ShapeDtypeStruct((m, n), F32),
        compiler_params=_cparams("parallel"),
        name="matmul_res",
    )(x, gg, w, h)


def _softplus(x):
    return jnp.maximum(x, 0.0) + jnp.log1p(jnp.exp(-jnp.abs(x)))


def _rwkv_pre_body(vres, x_ref, xp_ref, vf_ref, mu_ref, wr_ref, wk_ref, wv_ref,
                   w0_ref, w1_ref, w2_ref, a0_ref, a1_ref, a2_ref, g1_ref, g2_ref,
                   v0_ref, v1_ref, v2_ref, kk_ref, ka_ref,
                   r_o, lw_o, k_o, v_o, kkr_o, a_o, g_o):
    x = x_ref[...]
    xx = xp_ref[...] - x

    def lerp(c):
        return (x + xx * mu_ref[c:c + 1, :]).astype(BF16)

    r = _dot(lerp(0), wr_ref[...])
    k = _dot(lerp(1), wk_ref[...])
    xv = lerp(2)
    v = _dot(xv, wv_ref[...])
    w = w0_ref[...] + _dot(jnp.tanh(_dot(lerp(3), w1_ref[...])).astype(BF16), w2_ref[...])
    w = -_softplus(-w) - 0.5
    a = jax.nn.sigmoid(a0_ref[...] + _dot(_dot(lerp(4), a1_ref[...]).astype(BF16), a2_ref[...]))
    g = _dot(jax.nn.sigmoid(_dot(lerp(5), g1_ref[...])).astype(BF16), g2_ref[...])
    if vres:
        mix = jax.nn.sigmoid(v0_ref[...] + _dot(_dot(xv, v1_ref[...]).astype(BF16), v2_ref[...]))
        v = v + (vf_ref[...] - v) * mix
    r_o[...] = r
    lw_o[...] = -jnp.exp(w)
    k_o[...] = k * (1.0 + (a - 1.0) * ka_ref[...])
    v_o[...] = v
    kkr_o[...] = k * kk_ref[...]
    a_o[...] = a
    g_o[...] = g


def _pad_cols(w, n):
    return jnp.pad(w, ((0, 0), (0, n - w.shape[1])))


def _pad_rows(w, n):
    return jnp.pad(w, ((0, n - w.shape[0]), (0, 0)))


def _lora(w_down, w_up):
    r = w_down.shape[1]
    rp = -(-r // 128) * 128
    return _pad_cols(w_down, rp).astype(BF16), _pad_rows(w_up, rp).astype(BF16)


def _rwkv_pre(xn, xprev, v_first, p):
    m, d = xn.shape
    tm = _row_tile(m, 256)
    vres = v_first is not None
    vf = v_first if vres else xn
    w1, w2 = _lora(p["w1"], p["w2"])
    a1, a2 = _lora(p["a1"], p["a2"])
    g1, g2 = _lora(p["g1"], p["g2"])
    if vres:
        v1, v2 = _lora(p["v1"], p["v2"])
        v0 = p["v0"].reshape(1, d)
    else:
        v1, v2, v0 = a1, a2, p["a0"].reshape(1, d)
    tok = pl.BlockSpec((tm, d), lambda i: (i, 0))

    def full(a):
        return pl.BlockSpec(a.shape, lambda i: (0,) * a.ndim)

    wr, wk, wv = (p["w_rkv"][c].astype(BF16) for c in range(3))
    consts = [p["mu"], wr, wk, wv, p["w0"].reshape(1, d), w1, w2, p["a0"].reshape(1, d), a1, a2,
              g1, g2, v0, v1, v2, p["k_k"].reshape(1, d), p["k_a"].reshape(1, d)]
    return pl.pallas_call(
        functools.partial(_rwkv_pre_body, vres),
        grid=(m // tm,),
        in_specs=[tok, tok, tok] + [full(c) for c in consts],
        out_specs=[tok] * 7,
        out_shape=[jax.ShapeDtypeStruct((m, d), F32)] * 7,
        compiler_params=_cparams("parallel"),
        name="rwkv_pre",
    )(xn, xprev, vf, *consts)


def _split(x):
    hi = x.astype(BF16)
    lo = (x - hi.astype(F32)).astype(BF16)
    return hi, lo


def _mm3(dotfn, a, b):
    ah, al = _split(a)
    bh, bl = _split(b)
    return dotfn(ah, bh) + (dotfn(ah, bl) + dotfn(al, bh))


def _mm_exact_rhs(a, b01):
    a1 = a.astype(BF16)
    r1 = a - a1.astype(F32)
    a2 = r1.astype(BF16)
    a3 = (r1 - a2.astype(F32)).astype(BF16)
    return _dot(a1, b01) + (_dot(a2, b01) + _dot(a3, b01))


def _mm_exact_lhs(a01, b):
    b1 = b.astype(BF16)
    r1 = b - b1.astype(F32)
    b2 = r1.astype(BF16)
    b3 = (r1 - b2.astype(F32)).astype(BF16)
    return _dot(a01, b1) + (_dot(a01, b2) + _dot(a01, b3))


def _scan_body(L, n_heads, r_ref, lw_ref, k_ref, v_ref, kkr_ref, a_ref, rk_ref, lnw_ref, lnb_ref,
               s0_ref, y_ref, st_ref, s_sc):
    ci = pl.program_id(1)
    N = RWKV_HEAD

    @pl.when(ci == 0)
    def _():
        s_sc[...] = s0_ref[...]

    row = lax.broadcasted_iota(jnp.int32, (L, L), 0)
    col = lax.broadcasted_iota(jnp.int32, (L, L), 1)
    tri_incl = row >= col
    tri_strict = row > col
    same_sub = (row // SCAN_SUB) == (col // SCAN_SUB)
    ltri = tri_incl.astype(BF16)
    eye = (row == col).astype(F32)
    eye_n = (lax.broadcasted_iota(jnp.int32, (N, N), 0)
             == lax.broadcasted_iota(jnp.int32, (N, N), 1)).astype(F32)

    def pair(hp, carry):
        sl = pl.ds(pl.multiple_of(hp * 2 * N, 2 * N), 2 * N)
        lw2 = lw_ref[:, sl]
        c2 = _mm_exact_lhs(ltri, lw2)
        ec2 = jnp.exp(c2)
        eci2 = jnp.exp(-c2)
        ecp2 = jnp.exp(c2 - lw2)
        r2, k2, v2, kkr2, al2 = r_ref[:, sl], k_ref[:, sl], v_ref[:, sl], kkr_ref[:, sl], a_ref[:, sl]
        rk2, lnw2, lnb2 = rk_ref[:, sl], lnw_ref[:, sl], lnb_ref[:, sl]
        outs = []
        for j in range(2):
            h = hp * 2 + j
            cs = slice(j * N, (j + 1) * N)
            r, k, v, kkr, al = r2[:, cs], k2[:, cs], v2[:, cs], kkr2[:, cs], al2[:, cs]
            ec, eci, ecp = ec2[:, cs], eci2[:, cs], ecp2[:, cs]
            kk = kkr * lax.rsqrt(jnp.maximum(jnp.sum(kkr * kkr, axis=-1, keepdims=True), 1e-24))
            at = -kk * ecp
            bt = kk * al * eci
            kt = k * eci
            rt = r * ec
            nab = jnp.where(tri_strict, _mm3(_dot_nt, at, bt), 0.0)
            aak = jnp.where(tri_strict, _mm3(_dot_nt, at, kt), 0.0)
            gb = jnp.where(tri_incl, _mm3(_dot_nt, rt, bt), 0.0)
            gk = jnp.where(tri_incl, _mm3(_dot_nt, rt, kt), 0.0)
            nd = jnp.where(same_sub, nab, 0.0)
            noff = nab - nd
            nd2 = _mm3(_dot, nd, nd)
            nd4 = _mm3(_dot, nd2, nd2)
            nd8 = _mm3(_dot, nd4, nd4)
            td = eye + nd
            td = td + _mm3(_dot, td, nd2)
            td = td + _mm3(_dot, td, nd4)
            td = td + _mm3(_dot, td, nd8)
            rhs = jnp.concatenate([at, _mm3(_dot, aak, v)], axis=1)
            wz = _mm3(_dot, td, rhs)
            for _ in range(L // SCAN_SUB - 1):
                wz = _mm3(_dot, td, rhs + _mm3(_dot, noff, wz))
            gwz = _mm3(_dot, gb, wz)
            rp = rt + gwz[:, :N]
            y0 = gwz[:, N:] + _mm3(_dot, gk, v)
            ecl = ec[L - 1:L, :]
            wzb = _mm3(_dot_tn, wz, bt)
            pm = (eye_n + wzb[:N, :]) * ecl
            qm = (wzb[N:, :] + _mm3(_dot_tn, v, kt)) * ecl
            s_prev = s_sc[h]
            y = _mm3(_dot_nt, rp, s_prev) + y0
            s_sc[h] = _mm3(_dot, s_prev, pm) + qm
            mean = jnp.mean(y, axis=-1, keepdims=True)
            yc = y - mean
            var = jnp.mean(yc * yc, axis=-1, keepdims=True)
            yn = yc * lax.rsqrt(var + GN_EPS) * lnw2[:, cs] + lnb2[:, cs]
            bonus = jnp.sum(r * k * rk2[:, cs], axis=-1, keepdims=True) * v
            outs.append(yn + bonus)
        y_ref[:, sl] = jnp.concatenate(outs, axis=1)
        return carry

    lax.fori_loop(0, n_heads // 2, pair, 0)

    @pl.when(ci == pl.num_programs(1) - 1)
    def _():
        st_ref[...] = s_sc[...]


def _rwkv_scan(r, lw, k, v, kkr, a, r_k, lnx_w, lnx_b, s0, L):
    b, t, d = r.shape
    nh = d // RWKV_HEAD
    tok = pl.BlockSpec((None, L, d), lambda i, c: (i, c, 0))
    vec = pl.BlockSpec((1, d), lambda i, c: (0, 0))
    st = pl.BlockSpec((None, nh, RWKV_HEAD, RWKV_HEAD), lambda i, c: (i, 0, 0, 0))
    return pl.pallas_call(
        functools.partial(_scan_body, L, nh),
        grid=(b, t // L),
        in_specs=[tok] * 6 + [vec] * 3 + [st],
        out_specs=[tok, st],
        out_shape=[jax.ShapeDtypeStruct((b, t, d), F32),
                   jax.ShapeDtypeStruct((b, nh, RWKV_HEAD, RWKV_HEAD), F32)],
        scratch_shapes=[pltpu.VMEM((nh, RWKV_HEAD, RWKV_HEAD), F32)],
        compiler_params=_cparams("parallel", "arbitrary"),
        name="rwkv_scan",
    )(r, lw, k, v, kkr, a, r_k.reshape(1, d), lnx_w.reshape(1, d), lnx_b.reshape(1, d), s0)


def _compress_body(n_pages, nc, pt_ref, page_a_ref, page_b_ref, pos_ref, w1c_ref, w1f_ref, w2_ref,
                   o_ref, ch_sc):
    pg = pl.program_id(1)
    rows = page_a_ref.shape[0]
    per_page = rows // CMP_STRIDE
    base = pl.multiple_of(pg * per_page, per_page)
    for rr in range(CMP_STRIDE):
        for half, ref in enumerate((page_a_ref, page_b_ref)):
            xr = ref[pl.ds(rr, per_page, stride=CMP_STRIDE), :]
            for j in range(2):
                ch_sc[2 * half + j, pl.ds(base, per_page), rr * HEAD_DIM:(rr + 1) * HEAD_DIM] = (
                    xr[:, j * HEAD_DIM:(j + 1) * HEAD_DIM])

    @pl.when(pg == n_pages - 1)
    def _():
        posb = _dot(jnp.broadcast_to(pos_ref[...], (8, pos_ref.shape[1])).astype(BF16), w1f_ref[...])[0:1]
        last = lax.broadcasted_iota(jnp.int32, (nc, 1), 0) == nc - 1
        for hd in range(N_KV):
            hh = _dot(ch_sc[hd].astype(BF16), w1c_ref[...])
            lo = hh[:, :CMP_HID]
            hi = hh[:, CMP_HID:]
            hi = jnp.where(last, 0.0, jnp.concatenate([hi[1:], hi[:1]], axis=0))
            hid = jax.nn.gelu(lo + hi + posb)
            o_ref[:, hd * HEAD_DIM:(hd + 1) * HEAD_DIM] = _dot(hid.astype(BF16), w2_ref[...]).astype(BF16)


def _compress(src, col, page_table, pos, w1, w2):
    b, n_pages = page_table.shape
    rows = src.shape[1]
    nc = n_pages * rows // CMP_STRIDE
    half = CMP_STRIDE * HEAD_DIM
    w1c = jnp.concatenate([w1[:half], w1[half:]], axis=1).astype(BF16)
    grid_spec = pltpu.PrefetchScalarGridSpec(
        num_scalar_prefetch=1,
        grid=(b, n_pages),
        in_specs=[
            pl.BlockSpec((None, rows, KV_W // 2), lambda i, p, pt: (pt[i, p], 0, 2 * col)),
            pl.BlockSpec((None, rows, KV_W // 2), lambda i, p, pt: (pt[i, p], 0, 2 * col + 1)),
            pl.BlockSpec((1, 2 * half), lambda i, p, pt: (0, 0)),
            pl.BlockSpec((half, 2 * CMP_HID), lambda i, p, pt: (0, 0)),
            pl.BlockSpec((2 * half, CMP_HID), lambda i, p, pt: (0, 0)),
            pl.BlockSpec((CMP_HID, HEAD_DIM), lambda i, p, pt: (0, 0)),
        ],
        out_specs=pl.BlockSpec((None, nc, KV_W), lambda i, p, pt: (i, 0, 0)),
        scratch_shapes=[pltpu.VMEM((N_KV, nc, half), F32)],
    )
    return pl.pallas_call(
        functools.partial(_compress_body, n_pages, nc),
        grid_spec=grid_spec,
        out_shape=jax.ShapeDtypeStruct((b, nc, KV_W), BF16),
        compiler_params=_cparams("parallel", "arbitrary"),
        name="nsa_compress",
    )(page_table, src, src, pos.reshape(1, 2 * half), w1c, w1.astype(BF16), w2.astype(BF16))


def _query_rows(q, nq):
    pieces = []
    for kh in range(N_KV):
        for g in range(GROUP):
            hd = kh * GROUP + g
            x = q[:, hd * HEAD_DIM:(hd + 1) * HEAD_DIM] * (HEAD_DIM ** -0.5)
            parts = []
            if kh:
                parts.append(jnp.zeros((nq, kh * HEAD_DIM), F32))
            parts.append(x)
            if kh < N_KV - 1:
                parts.append(jnp.zeros((nq, (N_KV - 1 - kh) * HEAD_DIM), F32))
            pieces.append(jnp.concatenate(parts, axis=1))
    return jnp.concatenate(pieces, axis=0).astype(BF16)


def _row_meta(nq, t0):
    R = N_HEADS * nq
    rho = lax.broadcasted_iota(jnp.int32, (R, 1), 0)
    head = rho // nq
    slope = jnp.exp2(-8.0 * (head + 1).astype(F32) / N_HEADS)
    return slope, t0 + rho % nq


def _softmax_tiles(tiles):
    m = None
    for s, ok, _ in tiles:
        mt = jnp.max(jnp.where(ok, s, NEG_INF), axis=-1, keepdims=True)
        m = mt if m is None else jnp.maximum(m, mt)
    l = 0.0
    es = []
    for s, ok, _ in tiles:
        e = jnp.exp(jnp.where(ok, s, NEG_INF) - m)
        es.append(e)
        l = l + jnp.sum(e, axis=-1, keepdims=True)
    inv = 1.0 / l
    acc = 0.0
    ps = []
    for e, (s, ok, vv) in zip(es, tiles):
        p = jnp.where(ok, e * inv, 0.0)
        ps.append(p)
        if vv is not None:
            acc = acc + _dot(p.astype(BF16), vv)
    return acc, ps


def _select_blocks(p_c, nq, nc, ns, ns_pad, t_q):
    cstart = lax.broadcasted_iota(jnp.int32, (nc, ns_pad), 0) * CMP_STRIDE
    blk = lax.broadcasted_iota(jnp.int32, (nc, ns_pad), 1)
    overlap = ((cstart < (blk + 1) * SEL_BLOCK) & (cstart + CMP_LEN > blk * SEL_BLOCK)).astype(BF16)
    imps = []
    for kh in range(N_KV):
        acc = 0.0
        for g in range(GROUP):
            r0 = (kh * GROUP + g) * nq
            acc = acc + _dot(p_c[r0:r0 + nq].astype(BF16), overlap)
        imps.append(acc)
    imp = jnp.concatenate(imps, axis=0)
    lane = lax.broadcasted_iota(jnp.int32, imp.shape, 1)
    cur = t_q // SEL_BLOCK
    forced = (lane == 0) | (lane == cur) | (lane == cur - 1)
    imp = jnp.where(lane <= cur, jnp.where(forced, FORCED_SCORE, imp), -1.0)
    work = jnp.where(lane < ns, imp, REMOVED)
    sel = jnp.zeros(imp.shape, F32)
    for _ in range(min(N_SEL, ns)):
        mx = jnp.max(work, axis=-1, keepdims=True)
        first = jnp.min(jnp.where(work == mx, lane, ns_pad), axis=-1, keepdims=True)
        hit = lane == first
        sel = jnp.where(hit, 1.0, sel)
        work = jnp.where(hit, REMOVED, work)
    rows = []
    for kh in range(N_KV):
        rows += [sel[kh * nq:(kh + 1) * nq]] * GROUP
    return jnp.concatenate(rows, axis=0)


def _expand_sel(sel_rows, ns_pad, key0, n_keys):
    blk = lax.broadcasted_iota(jnp.int32, (ns_pad, n_keys), 0)
    kb = (key0 + lax.broadcasted_iota(jnp.int32, (ns_pad, n_keys), 1)) // SEL_BLOCK
    return _dot(sel_rows.astype(BF16), (blk == kb).astype(BF16)) > 0.5


def _compressed_branch(qrows, kc, vc, slope, t_row, nc):
    pos_end = lax.broadcasted_iota(jnp.int32, (1, nc), 1) * CMP_STRIDE + (CMP_LEN - 1)
    d_c = t_row - pos_end
    ok_c = d_c >= 0
    s_c = _dot_nt(qrows, kc) - slope * d_c.astype(F32)
    acc, ps = _softmax_tiles([(s_c, ok_c, vc)])
    return acc, ps[0]


def _combine(gates, accs, nq):
    outs = []
    for kh in range(N_KV):
        for g in range(GROUP):
            hd = kh * GROUP + g
            r0 = hd * nq
            o = 0.0
            for c in range(3):
                gcol = gates[:, 3 * hd + c:3 * hd + c + 1]
                o = o + gcol * accs[c][r0:r0 + nq, kh * HEAD_DIM:(kh + 1) * HEAD_DIM]
            outs.append(o)
    return jnp.concatenate(outs, axis=1)


def _attn_prompt_body(T, nc, ns, q_ref, gate_ref, kc_ref, vc_ref, ks_ref, vs_ref, kw_ref, vw_ref,
                      o_ref, m_sc, l_sc, acc_sc):
    nq = q_ref.shape[0]
    R = N_HEADS * nq
    i = pl.program_id(1)
    t0 = i * nq
    qrows = _query_rows(q_ref[...], nq)
    slope, t_row = _row_meta(nq, t0)

    acc_c, p_c = _compressed_branch(qrows, kc_ref[...], vc_ref[...], slope, t_row, nc)
    t_q = t0 + lax.broadcasted_iota(jnp.int32, (N_KV * nq, 1), 0) % nq
    sel_rows = _select_blocks(p_c, nq, nc, ns, ns, t_q)

    m_sc[...] = jnp.full(m_sc.shape, NEG_INF, F32)
    l_sc[...] = jnp.zeros_like(l_sc)
    acc_sc[...] = jnp.zeros_like(acc_sc)
    kt = min(KEY_TILE, T)

    def tile(j, carry):
        k0 = pl.multiple_of(j * kt, kt)
        pos = k0 + lax.broadcasted_iota(jnp.int32, (1, kt), 1)
        d = t_row - pos
        ok = (d >= 0) & _expand_sel(sel_rows, ns, k0, kt)
        s = _dot_nt(qrows, ks_ref[pl.ds(k0, kt), :]) - slope * d.astype(F32)
        s = jnp.where(ok, s, NEG_INF)
        m_old = m_sc[...]
        m_new = jnp.maximum(m_old, jnp.max(s, axis=-1, keepdims=True))
        p = jnp.where(ok, jnp.exp(s - m_new), 0.0)
        scale = jnp.exp(m_old - m_new)
        l_sc[...] = scale * l_sc[...] + jnp.sum(p, axis=-1, keepdims=True)
        acc_sc[...] = scale * acc_sc[...] + _dot(p.astype(BF16), vs_ref[pl.ds(k0, kt), :])
        m_sc[...] = m_new
        return carry

    lax.fori_loop(0, (t0 + nq + kt - 1) // kt, tile, 0)
    acc_s = acc_sc[...] * (1.0 / l_sc[...])

    nw = min(WINDOW + nq, T)
    w0 = jnp.clip(t0 - WINDOW, 0, T - nw)
    w0 = pl.multiple_of(w0, nq)
    posw = w0 + lax.broadcasted_iota(jnp.int32, (1, nw), 1)
    d_w = t_row - posw
    ok_w = (d_w >= 0) & (d_w <= WINDOW)
    s_w = _dot_nt(qrows, kw_ref[pl.ds(w0, nw), :]) - slope * d_w.astype(F32)
    acc_w, _ = _softmax_tiles([(s_w, ok_w, vw_ref[pl.ds(w0, nw), :])])

    gates = jax.nn.sigmoid(gate_ref[...])
    o_ref[...] = _combine(gates, (acc_c, acc_s, acc_w), nq)


def _attn_prompt(z, kvb, kc, vc, nq):
    b, t, _ = z.shape
    nqd = N_HEADS * HEAD_DIM
    nc = kc.shape[1]
    ns = t // SEL_BLOCK
    R = N_HEADS * nq
    kvspec = [pl.BlockSpec((None, t, KV_W), functools.partial(lambda c, i, j: (i, 0, c), c))
              for c in (2, 3, 4, 5)]
    return pl.pallas_call(
        functools.partial(_attn_prompt_body, t, nc, ns),
        grid=(b, t // nq),
        in_specs=[
            pl.BlockSpec((None, nq, nqd), lambda i, j: (i, j, 0)),
            pl.BlockSpec((None, nq, 128), lambda i, j: (i, j, nqd // 128)),
            pl.BlockSpec((None, nc, KV_W), lambda i, j: (i, 0, 0)),
            pl.BlockSpec((None, nc, KV_W), lambda i, j: (i, 0, 0)),
        ] + kvspec,
        out_specs=pl.BlockSpec((None, nq, nqd), lambda i, j: (i, j, 0)),
        out_shape=jax.ShapeDtypeStruct((b, t, nqd), F32),
        scratch_shapes=[pltpu.VMEM((R, 1), F32), pltpu.VMEM((R, 1), F32), pltpu.VMEM((R, KV_W), F32)],
        compiler_params=_cparams("parallel", "arbitrary"),
        name="nsa_attn_prompt",
    )(z, z, kc, vc, kvb, kvb, kvb, kvb)


def _attn_sample_body(n_pages, page, past, tq, nc, ns, ns_pad, pt_ref, q_ref, gate_ref, kc_ref, vc_ref,
                      *rest):
    kpages = rest[:n_pages]
    vpages = rest[n_pages:2 * n_pages]
    kn_ref, vn_ref, kwn_ref, vwn_ref, kwp_ref, vwp_ref, o_ref = rest[2 * n_pages:]
    nq = q_ref.shape[0]
    nn = kn_ref.shape[0]
    qrows = _query_rows(q_ref[...], nq)
    slope, t_row = _row_meta(nq, past)
    t_row = jnp.minimum(t_row, past + tq - 1)

    acc_c, p_c = _compressed_branch(qrows, kc_ref[...], vc_ref[...], slope, t_row, nc)
    t_q = jnp.minimum(past + lax.broadcasted_iota(jnp.int32, (N_KV * nq, 1), 0) % nq, past + tq - 1)
    sel_rows = _select_blocks(p_c, nq, nc, ns, ns_pad, t_q)

    def new_tile(k_ref, v_ref, sel):
        pos = past + lax.broadcasted_iota(jnp.int32, (1, nn), 1)
        d = t_row - pos
        ok = d >= 0
        if sel:
            ok = ok & _expand_sel(sel_rows, ns_pad, past, nn)
        s = _dot_nt(qrows, k_ref[...]) - slope * d.astype(F32)
        return s, ok, v_ref[...]

    tiles = []
    for p in range(n_pages):
        pos = p * page + lax.broadcasted_iota(jnp.int32, (1, page), 1)
        d = t_row - pos
        ok = (d >= 0) & _expand_sel(sel_rows, ns_pad, p * page, page)
        s = _dot_nt(qrows, kpages[p][...].astype(BF16)) - slope * d.astype(F32)
        tiles.append((s, ok, vpages[p][...].astype(BF16)))
    tiles.append(new_tile(kn_ref, vn_ref, True))
    acc_s, _ = _softmax_tiles(tiles)

    nwp = kwp_ref.shape[0]
    w_pos0 = past - nwp
    posw = w_pos0 + lax.broadcasted_iota(jnp.int32, (1, nwp), 1)
    d_w = t_row - posw
    ok_w = (d_w >= 0) & (d_w <= WINDOW)
    s_w = _dot_nt(qrows, kwp_ref[...].astype(BF16)) - slope * d_w.astype(F32)
    sn, okn, vn = new_tile(kwn_ref, vwn_ref, False)
    dn = t_row - (past + lax.broadcasted_iota(jnp.int32, (1, nn), 1))
    acc_w, _ = _softmax_tiles([(s_w, ok_w, vwp_ref[...].astype(BF16)), (sn, okn & (dn <= WINDOW), vn)])

    gates = jax.nn.sigmoid(gate_ref[...])
    o_ref[...] = _combine(gates, (acc_c, acc_s, acc_w), nq)


def _attn_sample(z, kvb, kc, vc, cache3, win3, page_table, tq):
    b, nq, _ = z.shape
    nqd = N_HEADS * HEAD_DIM
    n_pages = page_table.shape[1]
    page = cache3.shape[1]
    past = n_pages * page
    nn = kvb.shape[1]
    nc = kc.shape[1]
    ns = -(-(past + tq) // SEL_BLOCK)
    ns_pad = -(-ns // 128) * 128
    nwp = win3.shape[1]
    assert past % SEL_BLOCK == 0 and nn <= SEL_BLOCK and nwp == WINDOW
    kp = [pl.BlockSpec((None, page, KV_W), functools.partial(lambda p, i, pt: (pt[i, p], 0, 2), p))
          for p in range(n_pages)]
    vp = [pl.BlockSpec((None, page, KV_W), functools.partial(lambda p, i, pt: (pt[i, p], 0, 3), p))
          for p in range(n_pages)]
    newspec = [pl.BlockSpec((None, nn, KV_W), functools.partial(lambda c, i, pt: (i, 0, c), c))
               for c in (2, 3, 4, 5)]
    grid_spec = pltpu.PrefetchScalarGridSpec(
        num_scalar_prefetch=1,
        grid=(b,),
        in_specs=[
            pl.BlockSpec((None, nq, nqd), lambda i, pt: (i, 0, 0)),
            pl.BlockSpec((None, nq, 128), lambda i, pt: (i, 0, nqd // 128)),
            pl.BlockSpec((None, nc, KV_W), lambda i, pt: (i, 0, 0)),
            pl.BlockSpec((None, nc, KV_W), lambda i, pt: (i, 0, 0)),
        ] + kp + vp + newspec + [
            pl.BlockSpec((None, nwp, KV_W), lambda i, pt: (i, 0, 0)),
            pl.BlockSpec((None, nwp, KV_W), lambda i, pt: (i, 0, 1)),
        ],
        out_specs=pl.BlockSpec((None, nq, nqd), lambda i, pt: (i, 0, 0)),
    )
    return pl.pallas_call(
        functools.partial(_attn_sample_body, n_pages, page, past, tq, nc, ns, ns_pad),
        grid_spec=grid_spec,
        out_shape=jax.ShapeDtypeStruct((b, nq, nqd), F32),
        compiler_params=_cparams("parallel"),
        name="nsa_attn_sample",
    )(page_table, z, z, kc, vc, *([cache3] * (2 * n_pages)), kvb, kvb, kvb, kvb, win3, win3)


def kernel(x_prompt, x_sample, cache_cs, cache_win, state_shift, state_wkv, page_table, norm_g, ffn_w_in, ffn_w_out, rw_mu, rw_w_rkv, rw_w0, rw_w1, rw_w2, rw_a0, rw_a1, rw_a2, rw_g1, rw_g2, rw_k_k, rw_k_a, rw_r_k, rw_lnx_w, rw_lnx_b, rw_w_o, rw_v0, rw_v1, rw_v2, kv_norm_g, kv_w, cmp_pos, cmp_w1, cmp_w2, b_w_in, b_w_o, final_g):
    bp, tp, d = x_prompt.shape
    bs, ts, _ = x_sample.shape
    depth = norm_g.shape[0]
    n_a = rw_mu.shape[0]
    nh = d // RWKV_HEAD
    mp, ms = bp * tp, bs * ts
    n_pool, page = cache_cs.shape[:2]
    n_pages = page_table.shape[1]
    past = n_pages * page
    nqd = N_HEADS * HEAD_DIM
    assert tp % SCAN_CHUNK == 0 and tp % page == 0 and tp % SEL_BLOCK == 0
    nq_p = max(1, min(64, tp, 256 // bp))
    assert nq_p == SEL_BLOCK and ts <= 8
    assert (past + ts - CMP_LEN) // CMP_STRIDE * CMP_STRIDE + CMP_LEN <= past

    h = jnp.concatenate([x_prompt.reshape(mp, d), x_sample.reshape(ms, d)], axis=0)
    ts_pad = 8
    shifts_p, shifts_s, states_p, states_s = [], [], [], []
    v_first = None
    kv = kvb = kc_p = vc_p = kc_s = vc_s = None
    y = None
    xn = None
    for l in range(depth):
        g = norm_g[l]
        w_in0, w_out0 = ffn_w_in[l, 0].astype(BF16), ffn_w_out[l, 0].astype(BF16)
        w_in1, w_out1 = ffn_w_in[l, 1].astype(BF16), ffn_w_out[l, 1].astype(BF16)
        h, xn = _ffn(h, g[0], w_in0, w_out0, g_post=g[1])
        if l < n_a:
            xn_p = xn[:mp].reshape(bp, tp, d)
            xn_s = xn[mp:].reshape(bs, ts, d)
            shifts_p.append(xn_p[:, -1])
            shifts_s.append(xn_s[:, -1])
            xprev = jnp.concatenate([
                jnp.concatenate([jnp.zeros((bp, 1, d), F32), xn_p[:, :-1]], axis=1).reshape(mp, d),
                jnp.concatenate([state_shift[l][:, None, :], xn_s[:, :-1]], axis=1).reshape(ms, d)], axis=0)
            p = dict(mu=rw_mu[l], w_rkv=rw_w_rkv[l], w0=rw_w0[l], w1=rw_w1[l], w2=rw_w2[l], a0=rw_a0[l],
                     a1=rw_a1[l], a2=rw_a2[l], g1=rw_g1[l], g2=rw_g2[l], k_k=rw_k_k[l], k_a=rw_k_a[l])
            if l > 0:
                p.update(v0=rw_v0[l - 1], v1=rw_v1[l - 1], v2=rw_v2[l - 1])
            r, lw, k, v, kkr, a, gte = _rwkv_pre(xn, xprev, v_first, p)
            if l == 0:
                v_first = v
            seqs = (r, lw, k, v, kkr, a)
            yp, st_p = _rwkv_scan(*(t[:mp].reshape(bp, tp, d) for t in seqs),
                                  rw_r_k[l], rw_lnx_w[l], rw_lnx_b[l],
                                  jnp.zeros((bp, nh, RWKV_HEAD, RWKV_HEAD), F32), SCAN_CHUNK)
            padded = (jnp.pad(t[mp:].reshape(bs, ts, d), ((0, 0), (0, ts_pad - ts), (0, 0))) for t in seqs)
            ysm, st_s = _rwkv_scan(*padded, rw_r_k[l], rw_lnx_w[l], rw_lnx_b[l], state_wkv[l], ts_pad)
            states_p.append(st_p)
            states_s.append(st_s)
            ymix = jnp.concatenate([yp.reshape(mp, d), ysm[:, :ts].reshape(ms, d)], axis=0)
            h = _matmul_res(ymix, gte, rw_w_o[l].astype(BF16), h)
        else:
            lb = l - n_a
            ncol = b_w_in.shape[2]
            w_q = _pad_cols(b_w_in[lb], nqd + 128).astype(BF16)
            z, _ = _norm_matmul(xn, None, w_q)
            o_p = _attn_prompt(z[:mp].reshape(bp, tp, nqd + 128), kvb[:mp].reshape(bp, tp, 6 * KV_W),
                               kc_p, vc_p, nq_p)
            z_s = jnp.pad(z[mp:].reshape(bs, ts, nqd + 128), ((0, 0), (0, ts_pad - ts), (0, 0)))
            kvb_s = jnp.pad(kvb[mp:].reshape(bs, ts, 6 * KV_W), ((0, 0), (0, ts_pad - ts), (0, 0)))
            o_s = _attn_sample(z_s, kvb_s, kc_s, vc_s, cache_cs.reshape(n_pool, page, 4 * KV_W),
                               cache_win.reshape(bs, cache_win.shape[1], 2 * KV_W), page_table, ts)
            o = jnp.concatenate([o_p.reshape(mp, nqd), o_s[:, :ts].reshape(ms, nqd)], axis=0)
            h = _matmul_res(o, None, b_w_o[lb].astype(BF16), h)
        last = l == depth - 1
        res = _ffn(h, g[2], w_in1, w_out1, g_post=final_g if last else None)
        if last:
            h, y = res
        else:
            h = res
        if l == n_a - 1:
            kv, kvb = _norm_matmul(h, kv_norm_g, kv_w.astype(BF16))
            ident = jnp.arange(mp // page, dtype=jnp.int32).reshape(bp, tp // page)
            kv_pages = kv[:mp].reshape(mp // page, page, 6 * KV_W)
            kc_p = _compress(kv_pages, 0, ident, cmp_pos[0], cmp_w1[0], cmp_w2[0])
            vc_p = _compress(kv_pages, 1, ident, cmp_pos[1], cmp_w1[1], cmp_w2[1])
            cache3 = cache_cs.reshape(n_pool, page, 4 * KV_W)
            kc_s = _compress(cache3, 0, page_table, cmp_pos[0], cmp_w1[0], cmp_w2[0])
            vc_s = _compress(cache3, 1, page_table, cmp_pos[1], cmp_w1[1], cmp_w2[1])

    y_prompt = y[:mp].reshape(bp, tp, d)
    y_sample = y[mp:].reshape(bs, ts, d)
    kv_p = kv[:mp].reshape(bp, tp, 6, N_KV, HEAD_DIM)
    kv_s = kv[mp:].reshape(bs, ts, 6, N_KV, HEAD_DIM)
    p_cs = kv_p[:, :, :4]
    p_win = kv_p[:, tp - min(WINDOW, tp):, 4:]
    s_cs = kv_s[:, :, :4]
    win_all = jnp.concatenate([cache_win, kv_s[:, :, 4:]], axis=1)
    s_win = win_all[:, win_all.shape[1] - cache_win.shape[1]:]
    return (y_prompt, y_sample, jnp.stack(shifts_p), jnp.stack(states_p), p_cs, p_win,
            jnp.stack(shifts_s), jnp.stack(states_s), s_cs, s_win)
```

```python
import functools

import jax
import jax.numpy as jnp
from jax import lax
from jax.experimental import pallas as pl
from jax.experimental.pallas import tpu as pltpu

F32 = jnp.float32
BF16 = jnp.bfloat16

RWKV_HEAD = 64
GN_EPS = 64e-5
N_HEADS = 16
HEAD_DIM = 64
N_KV = 4
GROUP = N_HEADS // N_KV
KV_W = N_KV * HEAD_DIM
CMP_STRIDE = 16
CMP_LEN = 2 * CMP_STRIDE
CMP_HID = 2 * HEAD_DIM
SEL_BLOCK = 64
N_SEL = 16
WINDOW = 512
RMS_EPS = 1e-6
NEG_INF = -1e30
FORCED_SCORE = 1e4
REMOVED = -3e38

SCAN_CHUNK = 64
SCAN_SUB = 16
SCAN_HEADS = 16
P_GRAM, P_SOLVE, P_OUT, P_STATE = 1, 1, 1, 3
KEY_TILE = 512
VMEM_LIMIT = 56 * 1024 * 1024


def _cparams(*sem):
    return pltpu.CompilerParams(dimension_semantics=sem, vmem_limit_bytes=VMEM_LIMIT)


def _row_tile(m, cap):
    t = cap
    while t > 8 and m % t:
        t //= 2
    assert m % t == 0, (m, cap)
    return t


def _rms(x, g):
    return x * lax.rsqrt(jnp.mean(x * x, axis=-1, keepdims=True) + RMS_EPS) * g


def _dot(a, b):
    return jnp.dot(a, b, preferred_element_type=F32)


def _dot_nt(a, b):
    return lax.dot_general(a, b, (((1,), (1,)), ((), ())), preferred_element_type=F32)


def _dot_tn(a, b):
    return lax.dot_general(a, b, (((0,), (0,)), ((), ())), preferred_element_type=F32)


def _ffn_body(nf, post, h_ref, g_ref, wg_ref, wu_ref, wo_ref, gp_ref, *rest):
    if post:
        o_ref, on_ref, xn_sc, acc_sc = rest
    else:
        o_ref, xn_sc, acc_sc = rest
    f = pl.program_id(1)

    @pl.when(f == 0)
    def _():
        xn_sc[...] = _rms(h_ref[...], g_ref[...]).astype(BF16)
        acc_sc[...] = jnp.zeros_like(acc_sc)

    xn = xn_sc[...]
    gate = _dot(xn, wg_ref[...])
    up = _dot(xn, wu_ref[...])
    hid = (gate * jax.nn.sigmoid(gate) * up).astype(BF16)
    acc_sc[...] += _dot(hid, wo_ref[...])

    @pl.when(f == nf - 1)
    def _():
        hn = h_ref[...] + 0.5 * acc_sc[...]
        o_ref[...] = hn
        if post:
            on_ref[...] = _rms(hn, gp_ref[...])


def _ffn(h, g, w_in, w_out, g_post=None):
    m, d = h.shape
    dff = w_out.shape[0]
    tm = _row_tile(m, 512)
    tf = dff // 2 if (dff // 2) % 128 == 0 else dff
    nf = dff // tf
    post = g_post is not None
    gp = g_post if post else g
    out_shape = [jax.ShapeDtypeStruct((m, d), F32)]
    out_specs = [pl.BlockSpec((tm, d), lambda i, f: (i, 0))]
    if post:
        out_shape.append(jax.ShapeDtypeStruct((m, d), F32))
        out_specs.append(pl.BlockSpec((tm, d), lambda i, f: (i, 0)))
    res = pl.pallas_call(
        functools.partial(_ffn_body, nf, post),
        grid=(m // tm, nf),
        in_specs=[
            pl.BlockSpec((tm, d), lambda i, f: (i, 0)),
            pl.BlockSpec((1, d), lambda i, f: (0, 0)),
            pl.BlockSpec((d, tf), lambda i, f: (0, f)),
            pl.BlockSpec((d, tf), lambda i, f: (0, nf + f)),
            pl.BlockSpec((tf, d), lambda i, f: (f, 0)),
            pl.BlockSpec((1, d), lambda i, f: (0, 0)),
        ],
        out_specs=out_specs,
        out_shape=out_shape,
        scratch_shapes=[pltpu.VMEM((tm, d), BF16), pltpu.VMEM((tm, d), F32)],
        compiler_params=_cparams("parallel", "arbitrary"),
        name="ffn",
    )(h, g.reshape(1, d), w_in, w_in, w_out, gp.reshape(1, d))
    return res if post else res[0]


def _norm_matmul_body(norm, x_ref, g_ref, w_ref, o_ref, ob_ref):
    x = x_ref[...]
    if norm:
        x = _rms(x, g_ref[...])
    y = _dot(x.astype(BF16), w_ref[...])
    o_ref[...] = y
    ob_ref[...] = y.astype(BF16)


def _norm_matmul(x, g, w):
    m, d = x.shape
    n = w.shape[1]
    tm = _row_tile(m, 512)
    norm = g is not None
    gg = g.reshape(1, d) if norm else jnp.ones((1, d), F32)
    return pl.pallas_call(
        functools.partial(_norm_matmul_body, norm),
        grid=(m // tm,),
        in_specs=[
            pl.BlockSpec((tm, d), lambda i: (i, 0)),
            pl.BlockSpec((1, d), lambda i: (0, 0)),
            pl.BlockSpec((d, n), lambda i: (0, 0)),
        ],
        out_specs=[pl.BlockSpec((tm, n), lambda i: (i, 0)),
                   pl.BlockSpec((tm, n), lambda i: (i, 0))],
        out_shape=[jax.ShapeDtypeStruct((m, n), F32), jax.ShapeDtypeStruct((m, n), BF16)],
        compiler_params=_cparams("parallel"),
        name="norm_matmul",
    )(x, gg, w)


def _matmul_res_body(gated, x_ref, g_ref, w_ref, h_ref, o_ref):
    x = x_ref[...]
    if gated:
        x = x * g_ref[...]
    o_ref[...] = h_ref[...] + _dot(x.astype(BF16), w_ref[...])


def _matmul_res(x, gate, w, h):
    m, k = x.shape
    n = w.shape[1]
    tm = _row_tile(m, 512)
    gated = gate is not None
    gg = gate if gated else x
    return pl.pallas_call(
        functools.partial(_matmul_res_body, gated),
        grid=(m // tm,),
        in_specs=[
            pl.BlockSpec((tm, k), lambda i: (i, 0)),
            pl.BlockSpec((tm, k), lambda i: (i, 0)),
            pl.BlockSpec((k, n), lambda i: (0, 0)),
            pl.BlockSpec((tm, n), lambda i: (i, 0)),
        ],
        out_specs=pl.BlockSpec((tm, n), lambda i: (i, 0)),
        out_shape=jax.ShapeDtypeStruct((m, n), F32),
        compiler_params=_cparams("parallel"),
        name="matmul_res",
    )(x, gg, w, h)


def _softplus(x):
    return jnp.maximum(x, 0.0) + jnp.log1p(jnp.exp(-jnp.abs(x)))


def _rwkv_pre_body(vres, x_ref, xp_ref, vf_ref, mu_ref, wr_ref, wk_ref, wv_ref,
                   w0_ref, w1_ref, w2_ref, a0_ref, a1_ref, a2_ref, g1_ref, g2_ref,
                   v0_ref, v1_ref, v2_ref, kk_ref, ka_ref,
                   r_o, lw_o, k_o, v_o, kkr_o, a_o, g_o):
    x = x_ref[...]
    xx = xp_ref[...] - x

    def lerp(c):
        return (x + xx * mu_ref[c:c + 1, :]).astype(BF16)

    r = _dot(lerp(0), wr_ref[...])
    k = _dot(lerp(1), wk_ref[...])
    xv = lerp(2)
    v = _dot(xv, wv_ref[...])
    w = w0_ref[...] + _dot(jnp.tanh(_dot(lerp(3), w1_ref[...])).astype(BF16), w2_ref[...])
    w = -_softplus(-w) - 0.5
    a = jax.nn.sigmoid(a0_ref[...] + _dot(_dot(lerp(4), a1_ref[...]).astype(BF16), a2_ref[...]))
    g = _dot(jax.nn.sigmoid(_dot(lerp(5), g1_ref[...])).astype(BF16), g2_ref[...])
    if vres:
        mix = jax.nn.sigmoid(v0_ref[...] + _dot(_dot(xv, v1_ref[...]).astype(BF16), v2_ref[...]))
        v = v + (vf_ref[...] - v) * mix
    r_o[...] = r
    lw_o[...] = -jnp.exp(w)
    k_o[...] = k * (1.0 + (a - 1.0) * ka_ref[...])
    v_o[...] = v
    kkr_o[...] = k * kk_ref[...]
    a_o[...] = a
    g_o[...] = g


def _pad_cols(w, n):
    return jnp.pad(w, ((0, 0), (0, n - w.shape[1])))


def _pad_rows(w, n):
    return jnp.pad(w, ((0, n - w.shape[0]), (0, 0)))


def _lora(w_down, w_up):
    r = w_down.shape[1]
    rp = -(-r // 128) * 128
    return _pad_cols(w_down, rp).astype(BF16), _pad_rows(w_up, rp).astype(BF16)


def _rwkv_pre(xn, xprev, v_first, p):
    m, d = xn.shape
    tm = _row_tile(m, 256)
    vres = v_first is not None
    vf = v_first if vres else xn
    w1, w2 = _lora(p["w1"], p["w2"])
    a1, a2 = _lora(p["a1"], p["a2"])
    g1, g2 = _lora(p["g1"], p["g2"])
    if vres:
        v1, v2 = _lora(p["v1"], p["v2"])
        v0 = p["v0"].reshape(1, d)
    else:
        v1, v2, v0 = a1, a2, p["a0"].reshape(1, d)
    tok = pl.BlockSpec((tm, d), lambda i: (i, 0))

    def full(a):
        return pl.BlockSpec(a.shape, lambda i: (0,) * a.ndim)

    wr, wk, wv = (p["w_rkv"][c].astype(BF16) for c in range(3))
    consts = [p["mu"], wr, wk, wv, p["w0"].reshape(1, d), w1, w2, p["a0"].reshape(1, d), a1, a2,
              g1, g2, v0, v1, v2, p["k_k"].reshape(1, d), p["k_a"].reshape(1, d)]
    return pl.pallas_call(
        functools.partial(_rwkv_pre_body, vres),
        grid=(m // tm,),
        in_specs=[tok, tok, tok] + [full(c) for c in consts],
        out_specs=[tok] * 7,
        out_shape=[jax.ShapeDtypeStruct((m, d), F32)] * 7,
        compiler_params=_cparams("parallel"),
        name="rwkv_pre",
    )(xn, xprev, vf, *consts)


def _split(x):
    hi = x.astype(BF16)
    lo = (x - hi.astype(F32)).astype(BF16)
    return hi, lo


_BATCH_DIMS = {"nn": ((2,), (1,)), "nt": ((2,), (2,)), "tn": ((1,), (1,))}


def _bmm(kind, a, b, passes):
    dn = (_BATCH_DIMS[kind], ((0,), (0,)))

    def dg(x, y):
        return lax.dot_general(x, y, dn, preferred_element_type=F32)

    if passes == 1:
        return dg(a.astype(BF16), b.astype(BF16))
    ah, al = _split(a)
    bh, bl = _split(b)
    return dg(ah, bh) + (dg(ah, bl) + dg(al, bh))


def _mm_exact_lhs(a01, b):
    b1 = b.astype(BF16)
    r1 = b - b1.astype(F32)
    b2 = r1.astype(BF16)
    b3 = (r1 - b2.astype(F32)).astype(BF16)
    return _dot(a01, b1) + (_dot(a01, b2) + _dot(a01, b3))


def _scan_body(L, n_heads, r_ref, lw_ref, k_ref, v_ref, kkr_ref, a_ref, rk_ref, lnw_ref, lnb_ref,
               s0_ref, y_ref, st_ref, s_sc):
    ci = pl.program_id(1)
    N = RWKV_HEAD
    HB = min(SCAN_HEADS, n_heads)

    @pl.when(ci == 0)
    def _():
        s_sc[...] = s0_ref[...]

    row = lax.broadcasted_iota(jnp.int32, (1, L, L), 1)
    col = lax.broadcasted_iota(jnp.int32, (1, L, L), 2)
    tri_incl = row >= col
    tri_strict = row > col
    same_sub = (row // SCAN_SUB) == (col // SCAN_SUB)
    ltri = tri_incl[0].astype(BF16)
    eye = (row == col).astype(F32)
    eye_n = (lax.broadcasted_iota(jnp.int32, (1, N, N), 1)
             == lax.broadcasted_iota(jnp.int32, (1, N, N), 2)).astype(F32)

    lw = lw_ref[...]
    c = _mm_exact_lhs(ltri, lw)
    ec = jnp.exp(c)
    eci = jnp.exp(-c)
    ecp = jnp.exp(c - lw)
    r = r_ref[...]
    k = k_ref[...]
    kt_all = k * eci
    rt_all = r * ec
    be_all = a_ref[...] * eci
    rkr_all = r * k * rk_ref[...]

    for g in range(n_heads // HB):
        def heads(x):
            return jnp.stack([x[:, (g * HB + i) * N:(g * HB + i + 1) * N] for i in range(HB)])

        kkr = heads(kkr_ref[...])
        kk = kkr * lax.rsqrt(jnp.maximum(jnp.sum(kkr * kkr, axis=-1, keepdims=True), 1e-24))
        at = -kk * heads(ecp)
        bt = kk * heads(be_all)
        kt = heads(kt_all)
        rt = heads(rt_all)
        v = heads(v_ref[...])
        ab = jnp.concatenate([at, rt], axis=1)
        gram_b = _bmm("nt", ab, bt, P_GRAM)
        gram_k = _bmm("nt", ab, kt, P_GRAM)
        nab = jnp.where(tri_strict, gram_b[:, :L], 0.0)
        aak = jnp.where(tri_strict, gram_k[:, :L], 0.0)
        gb = jnp.where(tri_incl, gram_b[:, L:], 0.0)
        gk = jnp.where(tri_incl, gram_k[:, L:], 0.0)
        nd = jnp.where(same_sub, nab, 0.0)
        noff = nab - nd
        nd2 = _bmm("nn", nd, nd, P_SOLVE)
        nd4 = _bmm("nn", nd2, nd2, P_SOLVE)
        nd8 = _bmm("nn", nd4, nd4, P_SOLVE)
        td = eye + nd
        td = td + _bmm("nn", td, nd2, P_SOLVE)
        td = td + _bmm("nn", td, nd4, P_SOLVE)
        td = td + _bmm("nn", td, nd8, P_SOLVE)
        rhs = jnp.concatenate([at, _bmm("nn", aak, v, P_SOLVE)], axis=2)
        wz = _bmm("nn", td, rhs, P_SOLVE)
        for _ in range(L // SCAN_SUB - 1):
            wz = _bmm("nn", td, rhs + _bmm("nn", noff, wz, P_SOLVE), P_SOLVE)
        gwz = _bmm("nn", gb, wz, P_OUT)
        rp = rt + gwz[:, :, :N]
        y0 = gwz[:, :, N:] + _bmm("nn", gk, v, P_OUT)
        ecl = heads(ec[L - 1:L, :])
        wzb = _bmm("tn", wz, bt, P_STATE)
        pm = (eye_n + wzb[:, :N]) * ecl
        qm = (wzb[:, N:] + _bmm("tn", v, kt, P_STATE)) * ecl
        s_prev = s_sc[g * HB:(g + 1) * HB]
        y = _bmm("nt", rp, s_prev, P_OUT) + y0
        s_sc[g * HB:(g + 1) * HB] = _bmm("nn", s_prev, pm, P_STATE) + qm
        mean = jnp.mean(y, axis=-1, keepdims=True)
        yc = y - mean
        var = jnp.mean(yc * yc, axis=-1, keepdims=True)
        yn = yc * lax.rsqrt(var + GN_EPS) * heads(lnw_ref[...]) + heads(lnb_ref[...])
        out = yn + jnp.sum(heads(rkr_all), axis=-1, keepdims=True) * v
        y_ref[:, g * HB * N:(g + 1) * HB * N] = jnp.concatenate([out[i] for i in range(HB)], axis=1)

    @pl.when(ci == pl.num_programs(1) - 1)
    def _():
        st_ref[...] = s_sc[...]


def _rwkv_scan(r, lw, k, v, kkr, a, r_k, lnx_w, lnx_b, s0, L):
    b, t, d = r.shape
    nh = d // RWKV_HEAD
    tok = pl.BlockSpec((None, L, d), lambda i, c: (i, c, 0))
    vec = pl.BlockSpec((1, d), lambda i, c: (0, 0))
    st = pl.BlockSpec((None, nh, RWKV_HEAD, RWKV_HEAD), lambda i, c: (i, 0, 0, 0))
    return pl.pallas_call(
        functools.partial(_scan_body, L, nh),
        grid=(b, t // L),
        in_specs=[tok] * 6 + [vec] * 3 + [st],
        out_specs=[tok, st],
        out_shape=[jax.ShapeDtypeStruct((b, t, d), F32),
                   jax.ShapeDtypeStruct((b, nh, RWKV_HEAD, RWKV_HEAD), F32)],
        scratch_shapes=[pltpu.VMEM((nh, RWKV_HEAD, RWKV_HEAD), F32)],
        compiler_params=_cparams("parallel", "arbitrary"),
        name="rwkv_scan",
    )(r, lw, k, v, kkr, a, r_k.reshape(1, d), lnx_w.reshape(1, d), lnx_b.reshape(1, d), s0)


def _compress_body(n_pages, nc, pt_ref, page_a_ref, page_b_ref, pos_ref, w1c_ref, w1f_ref, w2_ref,
                   o_ref, ch_sc):
    pg = pl.program_id(1)
    rows = page_a_ref.shape[0]
    per_page = rows // CMP_STRIDE
    base = pl.multiple_of(pg * per_page, per_page)
    for rr in range(CMP_STRIDE):
        for half, ref in enumerate((page_a_ref, page_b_ref)):
            xr = ref[pl.ds(rr, per_page, stride=CMP_STRIDE), :]
            for j in range(2):
                ch_sc[2 * half + j, pl.ds(base, per_page), rr * HEAD_DIM:(rr + 1) * HEAD_DIM] = (
                    xr[:, j * HEAD_DIM:(j + 1) * HEAD_DIM])

    @pl.when(pg == n_pages - 1)
    def _():
        posb = _dot(jnp.broadcast_to(pos_ref[...], (8, pos_ref.shape[1])).astype(BF16), w1f_ref[...])[0:1]
        last = lax.broadcasted_iota(jnp.int32, (nc, 1), 0) == nc - 1
        for hd in range(N_KV):
            hh = _dot(ch_sc[hd].astype(BF16), w1c_ref[...])
            lo = hh[:, :CMP_HID]
            hi = hh[:, CMP_HID:]
            hi = jnp.where(last, 0.0, jnp.concatenate([hi[1:], hi[:1]], axis=0))
            hid = jax.nn.gelu(lo + hi + posb)
            o_ref[:, hd * HEAD_DIM:(hd + 1) * HEAD_DIM] = _dot(hid.astype(BF16), w2_ref[...]).astype(BF16)


def _compress(src, col, page_table, pos, w1, w2):
    b, n_pages = page_table.shape
    rows = src.shape[1]
    nc = n_pages * rows // CMP_STRIDE
    half = CMP_STRIDE * HEAD_DIM
    w1c = jnp.concatenate([w1[:half], w1[half:]], axis=1).astype(BF16)
    grid_spec = pltpu.PrefetchScalarGridSpec(
        num_scalar_prefetch=1,
        grid=(b, n_pages),
        in_specs=[
            pl.BlockSpec((None, rows, KV_W // 2), lambda i, p, pt: (pt[i, p], 0, 2 * col)),
            pl.BlockSpec((None, rows, KV_W // 2), lambda i, p, pt: (pt[i, p], 0, 2 * col + 1)),
            pl.BlockSpec((1, 2 * half), lambda i, p, pt: (0, 0)),
            pl.BlockSpec((half, 2 * CMP_HID), lambda i, p, pt: (0, 0)),
            pl.BlockSpec((2 * half, CMP_HID), lambda i, p, pt: (0, 0)),
            pl.BlockSpec((CMP_HID, HEAD_DIM), lambda i, p, pt: (0, 0)),
        ],
        out_specs=pl.BlockSpec((None, nc, KV_W), lambda i, p, pt: (i, 0, 0)),
        scratch_shapes=[pltpu.VMEM((N_KV, nc, half), F32)],
    )
    return pl.pallas_call(
        functools.partial(_compress_body, n_pages, nc),
        grid_spec=grid_spec,
        out_shape=jax.ShapeDtypeStruct((b, nc, KV_W), BF16),
        compiler_params=_cparams("parallel", "arbitrary"),
        name="nsa_compress",
    )(page_table, src, src, pos.reshape(1, 2 * half), w1c, w1.astype(BF16), w2.astype(BF16))


def _query_rows(q, nq):
    pieces = []
    for kh in range(N_KV):
        for g in range(GROUP):
            hd = kh * GROUP + g
            x = q[:, hd * HEAD_DIM:(hd + 1) * HEAD_DIM] * (HEAD_DIM ** -0.5)
            parts = []
            if kh:
                parts.append(jnp.zeros((nq, kh * HEAD_DIM), F32))
            parts.append(x)
            if kh < N_KV - 1:
                parts.append(jnp.zeros((nq, (N_KV - 1 - kh) * HEAD_DIM), F32))
            pieces.append(jnp.concatenate(parts, axis=1))
    return jnp.concatenate(pieces, axis=0).astype(BF16)


def _row_meta(nq, t0):
    R = N_HEADS * nq
    rho = lax.broadcasted_iota(jnp.int32, (R, 1), 0)
    head = rho // nq
    slope = jnp.exp2(-8.0 * (head + 1).astype(F32) / N_HEADS)
    return slope, t0 + rho % nq


def _softmax_tiles(tiles):
    m = None
    for s, ok, _ in tiles:
        mt = jnp.max(jnp.where(ok, s, NEG_INF), axis=-1, keepdims=True)
        m = mt if m is None else jnp.maximum(m, mt)
    l = 0.0
    es = []
    for s, ok, _ in tiles:
        e = jnp.exp(jnp.where(ok, s, NEG_INF) - m)
        es.append(e)
        l = l + jnp.sum(e, axis=-1, keepdims=True)
    inv = 1.0 / l
    acc = 0.0
    ps = []
    for e, (s, ok, vv) in zip(es, tiles):
        p = jnp.where(ok, e * inv, 0.0)
        ps.append(p)
        if vv is not None:
            acc = acc + _dot(p.astype(BF16), vv)
    return acc, ps


def _select_blocks(p_c, nq, nc, ns, ns_pad, t_q):
    cstart = lax.broadcasted_iota(jnp.int32, (nc, ns_pad), 0) * CMP_STRIDE
    blk = lax.broadcasted_iota(jnp.int32, (nc, ns_pad), 1)
    overlap = ((cstart < (blk + 1) * SEL_BLOCK) & (cstart + CMP_LEN > blk * SEL_BLOCK)).astype(BF16)
    imps = []
    for kh in range(N_KV):
        acc = 0.0
        for g in range(GROUP):
            r0 = (kh * GROUP + g) * nq
            acc = acc + _dot(p_c[r0:r0 + nq].astype(BF16), overlap)
        imps.append(acc)
    imp = jnp.concatenate(imps, axis=0)
    lane = lax.broadcasted_iota(jnp.int32, imp.shape, 1)
    cur = t_q // SEL_BLOCK
    forced = (lane == 0) | (lane == cur) | (lane == cur - 1)
    imp = jnp.where(lane <= cur, jnp.where(forced, FORCED_SCORE, imp), -1.0)
    work = jnp.where(lane < ns, imp, REMOVED)
    sel = jnp.zeros(imp.shape, F32)
    for _ in range(min(N_SEL, ns)):
        mx = jnp.max(work, axis=-1, keepdims=True)
        first = jnp.min(jnp.where(work == mx, lane, ns_pad), axis=-1, keepdims=True)
        hit = lane == first
        sel = jnp.where(hit, 1.0, sel)
        work = jnp.where(hit, REMOVED, work)
    rows = []
    for kh in range(N_KV):
        rows += [sel[kh * nq:(kh + 1) * nq]] * GROUP
    return jnp.concatenate(rows, axis=0)


def _expand_sel(sel_rows, ns_pad, key0, n_keys):
    blk = lax.broadcasted_iota(jnp.int32, (ns_pad, n_keys), 0)
    kb = (key0 + lax.broadcasted_iota(jnp.int32, (ns_pad, n_keys), 1)) // SEL_BLOCK
    return _dot(sel_rows.astype(BF16), (blk == kb).astype(BF16)) > 0.5


def _compressed_branch(qrows, kc, vc, slope, t_row, nc):
    pos_end = lax.broadcasted_iota(jnp.int32, (1, nc), 1) * CMP_STRIDE + (CMP_LEN - 1)
    d_c = t_row - pos_end
    ok_c = d_c >= 0
    s_c = _dot_nt(qrows, kc) - slope * d_c.astype(F32)
    acc, ps = _softmax_tiles([(s_c, ok_c, vc)])
    return acc, ps[0]


def _combine(gates, accs, nq):
    outs = []
    for kh in range(N_KV):
        for g in range(GROUP):
            hd = kh * GROUP + g
            r0 = hd * nq
            o = 0.0
            for c in range(3):
                gcol = gates[:, 3 * hd + c:3 * hd + c + 1]
                o = o + gcol * accs[c][r0:r0 + nq, kh * HEAD_DIM:(kh + 1) * HEAD_DIM]
            outs.append(o)
    return jnp.concatenate(outs, axis=1)


def _attn_prompt_body(T, nc, ns, q_ref, gate_ref, kc_ref, vc_ref, ks_ref, vs_ref, kw_ref, vw_ref,
                      o_ref, m_sc, l_sc, acc_sc):
    nq = q_ref.shape[0]
    R = N_HEADS * nq
    i = pl.program_id(1)
    t0 = i * nq
    qrows = _query_rows(q_ref[...], nq)
    slope, t_row = _row_meta(nq, t0)

    acc_c, p_c = _compressed_branch(qrows, kc_ref[...], vc_ref[...], slope, t_row, nc)
    t_q = t0 + lax.broadcasted_iota(jnp.int32, (N_KV * nq, 1), 0) % nq
    sel_rows = _select_blocks(p_c, nq, nc, ns, ns, t_q)

    m_sc[...] = jnp.full(m_sc.shape, NEG_INF, F32)
    l_sc[...] = jnp.zeros_like(l_sc)
    acc_sc[...] = jnp.zeros_like(acc_sc)
    kt = min(KEY_TILE, T)

    def tile(j, carry):
        k0 = pl.multiple_of(j * kt, kt)
        pos = k0 + lax.broadcasted_iota(jnp.int32, (1, kt), 1)
        d = t_row - pos
        ok = (d >= 0) & _expand_sel(sel_rows, ns, k0, kt)
        s = _dot_nt(qrows, ks_ref[pl.ds(k0, kt), :]) - slope * d.astype(F32)
        s = jnp.where(ok, s, NEG_INF)
        m_old = m_sc[...]
        m_new = jnp.maximum(m_old, jnp.max(s, axis=-1, keepdims=True))
        p = jnp.where(ok, jnp.exp(s - m_new), 0.0)
        scale = jnp.exp(m_old - m_new)
        l_sc[...] = scale * l_sc[...] + jnp.sum(p, axis=-1, keepdims=True)
        acc_sc[...] = scale * acc_sc[...] + _dot(p.astype(BF16), vs_ref[pl.ds(k0, kt), :])
        m_sc[...] = m_new
        return carry

    lax.fori_loop(0, (t0 + nq + kt - 1) // kt, tile, 0)
    acc_s = acc_sc[...] * (1.0 / l_sc[...])

    nw = min(WINDOW + nq, T)
    w0 = jnp.clip(t0 - WINDOW, 0, T - nw)
    w0 = pl.multiple_of(w0, nq)
    posw = w0 + lax.broadcasted_iota(jnp.int32, (1, nw), 1)
    d_w = t_row - posw
    ok_w = (d_w >= 0) & (d_w <= WINDOW)
    s_w = _dot_nt(qrows, kw_ref[pl.ds(w0, nw), :]) - slope * d_w.astype(F32)
    acc_w, _ = _softmax_tiles([(s_w, ok_w, vw_ref[pl.ds(w0, nw), :])])

    gates = jax.nn.sigmoid(gate_ref[...])
    o_ref[...] = _combine(gates, (acc_c, acc_s, acc_w), nq)


def _attn_prompt(z, kvb, kc, vc, nq):
    b, t, _ = z.shape
    nqd = N_HEADS * HEAD_DIM
    nc = kc.shape[1]
    ns = t // SEL_BLOCK
    R = N_HEADS * nq
    kvspec = [pl.BlockSpec((None, t, KV_W), functools.partial(lambda c, i, j: (i, 0, c), c))
              for c in (2, 3, 4, 5)]
    return pl.pallas_call(
        functools.partial(_attn_prompt_body, t, nc, ns),
        grid=(b, t // nq),
        in_specs=[
            pl.BlockSpec((None, nq, nqd), lambda i, j: (i, j, 0)),
            pl.BlockSpec((None, nq, 128), lambda i, j: (i, j, nqd // 128)),
            pl.BlockSpec((None, nc, KV_W), lambda i, j: (i, 0, 0)),
            pl.BlockSpec((None, nc, KV_W), lambda i, j: (i, 0, 0)),
        ] + kvspec,
        out_specs=pl.BlockSpec((None, nq, nqd), lambda i, j: (i, j, 0)),
        out_shape=jax.ShapeDtypeStruct((b, t, nqd), F32),
        scratch_shapes=[pltpu.VMEM((R, 1), F32), pltpu.VMEM((R, 1), F32), pltpu.VMEM((R, KV_W), F32)],
        compiler_params=_cparams("parallel", "arbitrary"),
        name="nsa_attn_prompt",
    )(z, z, kc, vc, kvb, kvb, kvb, kvb)


def _attn_sample_body(n_pages, page, past, tq, nc, ns, ns_pad, pt_ref, q_ref, gate_ref, kc_ref, vc_ref,
                      *rest):
    kpages = rest[:n_pages]
    vpages = rest[n_pages:2 * n_pages]
    kn_ref, vn_ref, kwn_ref, vwn_ref, kwp_ref, vwp_ref, o_ref = rest[2 * n_pages:]
    nq = q_ref.shape[0]
    nn = kn_ref.shape[0]
    qrows = _query_rows(q_ref[...], nq)
    slope, t_row = _row_meta(nq, past)
    t_row = jnp.minimum(t_row, past + tq - 1)

    acc_c, p_c = _compressed_branch(qrows, kc_ref[...], vc_ref[...], slope, t_row, nc)
    t_q = jnp.minimum(past + lax.broadcasted_iota(jnp.int32, (N_KV * nq, 1), 0) % nq, past + tq - 1)
    sel_rows = _select_blocks(p_c, nq, nc, ns, ns_pad, t_q)

    def new_tile(k_ref, v_ref, sel):
        pos = past + lax.broadcasted_iota(jnp.int32, (1, nn), 1)
        d = t_row - pos
        ok = d >= 0
        if sel:
            ok = ok & _expand_sel(sel_rows, ns_pad, past, nn)
        s = _dot_nt(qrows, k_ref[...]) - slope * d.astype(F32)
        return s, ok, v_ref[...]

    tiles = []
    for p in range(n_pages):
        pos = p * page + lax.broadcasted_iota(jnp.int32, (1, page), 1)
        d = t_row - pos
        ok = (d >= 0) & _expand_sel(sel_rows, ns_pad, p * page, page)
        s = _dot_nt(qrows, kpages[p][...].astype(BF16)) - slope * d.astype(F32)
        tiles.append((s, ok, vpages[p][...].astype(BF16)))
    tiles.append(new_tile(kn_ref, vn_ref, True))
    acc_s, _ = _softmax_tiles(tiles)

    nwp = kwp_ref.shape[0]
    w_pos0 = past - nwp
    posw = w_pos0 + lax.broadcasted_iota(jnp.int32, (1, nwp), 1)
    d_w = t_row - posw
    ok_w = (d_w >= 0) & (d_w <= WINDOW)
    s_w = _dot_nt(qrows, kwp_ref[...].astype(BF16)) - slope * d_w.astype(F32)
    sn, okn, vn = new_tile(kwn_ref, vwn_ref, False)
    dn = t_row - (past + lax.broadcasted_iota(jnp.int32, (1, nn), 1))
    acc_w, _ = _softmax_tiles([(s_w, ok_w, vwp_ref[...].astype(BF16)), (sn, okn & (dn <= WINDOW), vn)])

    gates = jax.nn.sigmoid(gate_ref[...])
    o_ref[...] = _combine(gates, (acc_c, acc_s, acc_w), nq)


def _attn_sample(z, kvb, kc, vc, cache3, win3, page_table, tq):
    b, nq, _ = z.shape
    nqd = N_HEADS * HEAD_DIM
    n_pages = page_table.shape[1]
    page = cache3.shape[1]
    past = n_pages * page
    nn = kvb.shape[1]
    nc = kc.shape[1]
    ns = -(-(past + tq) // SEL_BLOCK)
    ns_pad = -(-ns // 128) * 128
    nwp = win3.shape[1]
    assert past % SEL_BLOCK == 0 and nn <= SEL_BLOCK and nwp == WINDOW
    kp = [pl.BlockSpec((None, page, KV_W), functools.partial(lambda p, i, pt: (pt[i, p], 0, 2), p))
          for p in range(n_pages)]
    vp = [pl.BlockSpec((None, page, KV_W), functools.partial(lambda p, i, pt: (pt[i, p], 0, 3), p))
          for p in range(n_pages)]
    newspec = [pl.BlockSpec((None, nn, KV_W), functools.partial(lambda c, i, pt: (i, 0, c), c))
               for c in (2, 3, 4, 5)]
    grid_spec = pltpu.PrefetchScalarGridSpec(
        num_scalar_prefetch=1,
        grid=(b,),
        in_specs=[
            pl.BlockSpec((None, nq, nqd), lambda i, pt: (i, 0, 0)),
            pl.BlockSpec((None, nq, 128), lambda i, pt: (i, 0, nqd // 128)),
            pl.BlockSpec((None, nc, KV_W), lambda i, pt: (i, 0, 0)),
            pl.BlockSpec((None, nc, KV_W), lambda i, pt: (i, 0, 0)),
        ] + kp + vp + newspec + [
            pl.BlockSpec((None, nwp, KV_W), lambda i, pt: (i, 0, 0)),
            pl.BlockSpec((None, nwp, KV_W), lambda i, pt: (i, 0, 1)),
        ],
        out_specs=pl.BlockSpec((None, nq, nqd), lambda i, pt: (i, 0, 0)),
    )
    return pl.pallas_call(
        functools.partial(_attn_sample_body, n_pages, page, past, tq, nc, ns, ns_pad),
        grid_spec=grid_spec,
        out_shape=jax.ShapeDtypeStruct((b, nq, nqd), F32),
        compiler_params=_cparams("parallel"),
        name="nsa_attn_sample",
    )(page_table, z, z, kc, vc, *([cache3] * (2 * n_pages)), kvb, kvb, kvb, kvb, win3, win3)


def kernel(x_prompt, x_sample, cache_cs, cache_win, state_shift, state_wkv, page_table, norm_g, ffn_w_in, ffn_w_out, rw_mu, rw_w_rkv, rw_w0, rw_w1, rw_w2, rw_a0, rw_a1, rw_a2, rw_g1, rw_g2, rw_k_k, rw_k_a, rw_r_k, rw_lnx_w, rw_lnx_b, rw_w_o, rw_v0, rw_v1, rw_v2, kv_norm_g, kv_w, cmp_pos, cmp_w1, cmp_w2, b_w_in, b_w_o, final_g):
    bp, tp, d = x_prompt.shape
    bs, ts, _ = x_sample.shape
    depth = norm_g.shape[0]
    n_a = rw_mu.shape[0]
    nh = d // RWKV_HEAD
    mp, ms = bp * tp, bs * ts
    n_pool, page = cache_cs.shape[:2]
    n_pages = page_table.shape[1]
    past = n_pages * page
    nqd = N_HEADS * HEAD_DIM
    assert tp % SCAN_CHUNK == 0 and tp % page == 0 and tp % SEL_BLOCK == 0
    nq_p = max(1, min(64, tp, 256 // bp))
    assert nq_p == SEL_BLOCK and ts <= 8
    assert (past + ts - CMP_LEN) // CMP_STRIDE * CMP_STRIDE + CMP_LEN <= past

    h = jnp.concatenate([x_prompt.reshape(mp, d), x_sample.reshape(ms, d)], axis=0)
    ts_pad = 8
    shifts_p, shifts_s, states_p, states_s = [], [], [], []
    v_first = None
    kv = kvb = kc_p = vc_p = kc_s = vc_s = None
    y = None
    xn = None
    for l in range(depth):
        g = norm_g[l]
        w_in0, w_out0 = ffn_w_in[l, 0].astype(BF16), ffn_w_out[l, 0].astype(BF16)
        w_in1, w_out1 = ffn_w_in[l, 1].astype(BF16), ffn_w_out[l, 1].astype(BF16)
        h, xn = _ffn(h, g[0], w_in0, w_out0, g_post=g[1])
        if l < n_a:
            xn_p = xn[:mp].reshape(bp, tp, d)
            xn_s = xn[mp:].reshape(bs, ts, d)
            shifts_p.append(xn_p[:, -1])
            shifts_s.append(xn_s[:, -1])
            xprev = jnp.concatenate([
                jnp.concatenate([jnp.zeros((bp, 1, d), F32), xn_p[:, :-1]], axis=1).reshape(mp, d),
                jnp.concatenate([state_shift[l][:, None, :], xn_s[:, :-1]], axis=1).reshape(ms, d)], axis=0)
            p = dict(mu=rw_mu[l], w_rkv=rw_w_rkv[l], w0=rw_w0[l], w1=rw_w1[l], w2=rw_w2[l], a0=rw_a0[l],
                     a1=rw_a1[l], a2=rw_a2[l], g1=rw_g1[l], g2=rw_g2[l], k_k=rw_k_k[l], k_a=rw_k_a[l])
            if l > 0:
                p.update(v0=rw_v0[l - 1], v1=rw_v1[l - 1], v2=rw_v2[l - 1])
            r, lw, k, v, kkr, a, gte = _rwkv_pre(xn, xprev, v_first, p)
            if l == 0:
                v_first = v
            seqs = (r, lw, k, v, kkr, a)
            yp, st_p = _rwkv_scan(*(t[:mp].reshape(bp, tp, d) for t in seqs),
                                  rw_r_k[l], rw_lnx_w[l], rw_lnx_b[l],
                                  jnp.zeros((bp, nh, RWKV_HEAD, RWKV_HEAD), F32), SCAN_CHUNK)
            padded = (jnp.pad(t[mp:].reshape(bs, ts, d), ((0, 0), (0, ts_pad - ts), (0, 0))) for t in seqs)
            ysm, st_s = _rwkv_scan(*padded, rw_r_k[l], rw_lnx_w[l], rw_lnx_b[l], state_wkv[l], ts_pad)
            states_p.append(st_p)
            states_s.append(st_s)
            ymix = jnp.concatenate([yp.reshape(mp, d), ysm[:, :ts].reshape(ms, d)], axis=0)
            h = _matmul_res(ymix, gte, rw_w_o[l].astype(BF16), h)
        else:
            lb = l - n_a
            ncol = b_w_in.shape[2]
            w_q = _pad_cols(b_w_in[lb], nqd + 128).astype(BF16)
            z, _ = _norm_matmul(xn, None, w_q)
            o_p = _attn_prompt(z[:mp].reshape(bp, tp, nqd + 128), kvb[:mp].reshape(bp, tp, 6 * KV_W),
                               kc_p, vc_p, nq_p)
            z_s = jnp.pad(z[mp:].reshape(bs, ts, nqd + 128), ((0, 0), (0, ts_pad - ts), (0, 0)))
            kvb_s = jnp.pad(kvb[mp:].reshape(bs, ts, 6 * KV_W), ((0, 0), (0, ts_pad - ts), (0, 0)))
            o_s = _attn_sample(z_s, kvb_s, kc_s, vc_s, cache_cs.reshape(n_pool, page, 4 * KV_W),
                               cache_win.reshape(bs, cache_win.shape[1], 2 * KV_W), page_table, ts)
            o = jnp.concatenate([o_p.reshape(mp, nqd), o_s[:, :ts].reshape(ms, nqd)], axis=0)
            h = _matmul_res(o, None, b_w_o[lb].astype(BF16), h)
        last = l == depth - 1
        res = _ffn(h, g[2], w_in1, w_out1, g_post=final_g if last else None)
        if last:
            h, y = res
        else:
            h = res
        if l == n_a - 1:
            kv, kvb = _norm_matmul(h, kv_norm_g, kv_w.astype(BF16))
            ident = jnp.arange(mp // page, dtype=jnp.int32).reshape(bp, tp // page)
            kv_pages = kv[:mp].reshape(mp // page, page, 6 * KV_W)
            kc_p = _compress(kv_pages, 0, ident, cmp_pos[0], cmp_w1[0], cmp_w2[0])
            vc_p = _compress(kv_pages, 1, ident, cmp_pos[1], cmp_w1[1], cmp_w2[1])
            cache3 = cache_cs.reshape(n_pool, page, 4 * KV_W)
            kc_s = _compress(cache3, 0, page_table, cmp_pos[0], cmp_w1[0], cmp_w2[0])
            vc_s = _compress(cache3, 1, page_table, cmp_pos[1], cmp_w1[1], cmp_w2[1])

    y_prompt = y[:mp].reshape(bp, tp, d)
    y_sample = y[mp:].reshape(bs, ts, d)
    kv_p = kv[:mp].reshape(bp, tp, 6, N_KV, HEAD_DIM)
    kv_s = kv[mp:].reshape(bs, ts, 6, N_KV, HEAD_DIM)
    p_cs = kv_p[:, :, :4]
    p_win = kv_p[:, tp - min(WINDOW, tp):, 4:]
    s_cs = kv_s[:, :, :4]
    win_all = jnp.concatenate([cache_win, kv_s[:, :, 4:]], axis=1)
    s_win = win_all[:, win_all.shape[1] - cache_win.shape[1]:]
    return (y_prompt, y_sample, jnp.stack(shifts_p), jnp.stack(states_p), p_cs, p_win,
            jnp.stack(shifts_s), jnp.stack(states_s), s_cs, s_win)
```

```python
import functools

import jax
import jax.numpy as jnp
from jax import lax
from jax.experimental import pallas as pl
from jax.experimental.pallas import tpu as pltpu

F32 = jnp.float32
BF16 = jnp.bfloat16

RWKV_HEAD = 64
GN_EPS = 64e-5
N_HEADS = 16
HEAD_DIM = 64
N_KV = 4
GROUP = N_HEADS // N_KV
KV_W = N_KV * HEAD_DIM
CMP_STRIDE = 16
CMP_LEN = 2 * CMP_STRIDE
CMP_HID = 2 * HEAD_DIM
SEL_BLOCK = 64
N_SEL = 16
WINDOW = 512
RMS_EPS = 1e-6
NEG_INF = -1e30
FORCED_SCORE = 1e4
REMOVED = -3e38

SCAN_CHUNK = 64
SCAN_SUB = 16
SCAN_HEADS = 16
P_GRAM, P_SOLVE, P_OUT, P_STATE = 1, 1, 1, 3
KEY_TILE = 512
VMEM_LIMIT = 56 * 1024 * 1024


def _cparams(*sem):
    return pltpu.CompilerParams(dimension_semantics=sem, vmem_limit_bytes=VMEM_LIMIT)


def _row_tile(m, cap):
    t = cap
    while t > 8 and m % t:
        t //= 2
    assert m % t == 0, (m, cap)
    return t


def _rms(x, g):
    return x * lax.rsqrt(jnp.mean(x * x, axis=-1, keepdims=True) + RMS_EPS) * g


def _dot(a, b):
    return jnp.dot(a, b, preferred_element_type=F32)


def _dot_nt(a, b):
    return lax.dot_general(a, b, (((1,), (1,)), ((), ())), preferred_element_type=F32)


def _dot_tn(a, b):
    return lax.dot_general(a, b, (((0,), (0,)), ((), ())), preferred_element_type=F32)


def _ffn_body(nf, post, h_ref, g_ref, wg_ref, wu_ref, wo_ref, gp_ref, *rest):
    if post:
        o_ref, on_ref, xn_sc, acc_sc = rest
    else:
        o_ref, xn_sc, acc_sc = rest
    f = pl.program_id(1)

    @pl.when(f == 0)
    def _():
        xn_sc[...] = _rms(h_ref[...], g_ref[...]).astype(BF16)
        acc_sc[...] = jnp.zeros_like(acc_sc)

    xn = xn_sc[...]
    gate = _dot(xn, wg_ref[...])
    up = _dot(xn, wu_ref[...])
    hid = (gate * jax.nn.sigmoid(gate) * up).astype(BF16)
    acc_sc[...] += _dot(hid, wo_ref[...])

    @pl.when(f == nf - 1)
    def _():
        hn = h_ref[...] + 0.5 * acc_sc[...]
        o_ref[...] = hn
        if post:
            on_ref[...] = _rms(hn, gp_ref[...])


def _ffn(h, g, w_in, w_out, g_post=None):
    m, d = h.shape
    dff = w_out.shape[0]
    tm = _row_tile(m, 512)
    tf = dff // 2 if (dff // 2) % 128 == 0 else dff
    nf = dff // tf
    post = g_post is not None
    gp = g_post if post else g
    out_shape = [jax.ShapeDtypeStruct((m, d), F32)]
    out_specs = [pl.BlockSpec((tm, d), lambda i, f: (i, 0))]
    if post:
        out_shape.append(jax.ShapeDtypeStruct((m, d), F32))
        out_specs.append(pl.BlockSpec((tm, d), lambda i, f: (i, 0)))
    res = pl.pallas_call(
        functools.partial(_ffn_body, nf, post),
        grid=(m // tm, nf),
        in_specs=[
            pl.BlockSpec((tm, d), lambda i, f: (i, 0)),
            pl.BlockSpec((1, d), lambda i, f: (0, 0)),
            pl.BlockSpec((d, tf), lambda i, f: (0, f)),
            pl.BlockSpec((d, tf), lambda i, f: (0, nf + f)),
            pl.BlockSpec((tf, d), lambda i, f: (f, 0)),
            pl.BlockSpec((1, d), lambda i, f: (0, 0)),
        ],
        out_specs=out_specs,
        out_shape=out_shape,
        scratch_shapes=[pltpu.VMEM((tm, d), BF16), pltpu.VMEM((tm, d), F32)],
        compiler_params=_cparams("parallel", "arbitrary"),
        name="ffn",
    )(h, g.reshape(1, d), w_in, w_in, w_out, gp.reshape(1, d))
    return res if post else res[0]


def _norm_matmul_body(norm, x_ref, g_ref, w_ref, o_ref, ob_ref):
    x = x_ref[...]
    if norm:
        x = _rms(x, g_ref[...])
    y = _dot(x.astype(BF16), w_ref[...])
    o_ref[...] = y
    ob_ref[...] = y.astype(BF16)


def _norm_matmul(x, g, w):
    m, d = x.shape
    n = w.shape[1]
    tm = _row_tile(m, 512)
    norm = g is not None
    gg = g.reshape(1, d) if norm else jnp.ones((1, d), F32)
    return pl.pallas_call(
        functools.partial(_norm_matmul_body, norm),
        grid=(m // tm,),
        in_specs=[
            pl.BlockSpec((tm, d), lambda i: (i, 0)),
            pl.BlockSpec((1, d), lambda i: (0, 0)),
            pl.BlockSpec((d, n), lambda i: (0, 0)),
        ],
        out_specs=[pl.BlockSpec((tm, n), lambda i: (i, 0)),
                   pl.BlockSpec((tm, n), lambda i: (i, 0))],
        out_shape=[jax.ShapeDtypeStruct((m, n), F32), jax.ShapeDtypeStruct((m, n), BF16)],
        compiler_params=_cparams("parallel"),
        name="norm_matmul",
    )(x, gg, w)


def _matmul_res_body(gated, x_ref, g_ref, w_ref, h_ref, o_ref):
    x = x_ref[...]
    if gated:
        x = x * g_ref[...]
    o_ref[...] = h_ref[...] + _dot(x.astype(BF16), w_ref[...])


def _matmul_res(x, gate, w, h):
    m, k = x.shape
    n = w.shape[1]
    tm = _row_tile(m, 512)
    gated = gate is not None
    gg = gate if gated else x
    return pl.pallas_call(
        functools.partial(_matmul_res_body, gated),
        grid=(m // tm,),
        in_specs=[
            pl.BlockSpec((tm, k), lambda i: (i, 0)),
            pl.BlockSpec((tm, k), lambda i: (i, 0)),
            pl.BlockSpec((k, n), lambda i: (0, 0)),
            pl.BlockSpec((tm, n), lambda i: (i, 0)),
        ],
        out_specs=pl.BlockSpec((tm, n), lambda i: (i, 0)),
        out_shape=jax.ShapeDtypeStruct((m, n), F32),
        compiler_params=_cparams("parallel"),
        name="matmul_res",
    )(x, gg, w, h)


def _softplus(x):
    return jnp.maximum(x, 0.0) + jnp.log1p(jnp.exp(-jnp.abs(x)))


def _rwkv_pre_body(vres, shift_rows, x_ref, xp_ref, vf_ref, mu_ref, wr_ref, wk_ref, wv_ref,
                   w0_ref, w1_ref, w2_ref, a0_ref, a1_ref, a2_ref, g1_ref, g2_ref,
                   v0_ref, v1_ref, v2_ref, kk_ref, ka_ref,
                   r_o, lw_o, k_o, v_o, kkr_o, a_o, g_o):
    x = x_ref[...]
    if shift_rows:
        first = lax.broadcasted_iota(jnp.int32, x.shape, 0) == 0
        xp = jnp.where(first, xp_ref[...], pltpu.roll(x, 1, axis=0))
    else:
        xp = xp_ref[...]
    xx = xp - x

    def lerp(c):
        return (x + xx * mu_ref[c:c + 1, :]).astype(BF16)

    r = _dot(lerp(0), wr_ref[...])
    k = _dot(lerp(1), wk_ref[...])
    xv = lerp(2)
    v = _dot(xv, wv_ref[...])
    w = w0_ref[...] + _dot(jnp.tanh(_dot(lerp(3), w1_ref[...])).astype(BF16), w2_ref[...])
    w = -_softplus(-w) - 0.5
    a = jax.nn.sigmoid(a0_ref[...] + _dot(_dot(lerp(4), a1_ref[...]).astype(BF16), a2_ref[...]))
    g = _dot(jax.nn.sigmoid(_dot(lerp(5), g1_ref[...])).astype(BF16), g2_ref[...])
    if vres:
        mix = jax.nn.sigmoid(v0_ref[...] + _dot(_dot(xv, v1_ref[...]).astype(BF16), v2_ref[...]))
        v = v + (vf_ref[...] - v) * mix
    r_o[...] = r
    lw_o[...] = -jnp.exp(w)
    k_o[...] = k * (1.0 + (a - 1.0) * ka_ref[...])
    v_o[...] = v
    kkr_o[...] = k * kk_ref[...]
    a_o[...] = a
    g_o[...] = g


def _pad_cols(w, n):
    return jnp.pad(w, ((0, 0), (0, n - w.shape[1])))


def _pad_rows(w, n):
    return jnp.pad(w, ((0, n - w.shape[0]), (0, 0)))


def _lora(w_down, w_up):
    r = w_down.shape[1]
    rp = -(-r // 128) * 128
    return _pad_cols(w_down, rp).astype(BF16), _pad_rows(w_up, rp).astype(BF16)


def _rwkv_pre(xn, xprev, v_first, p, seq_len=None):
    m, d = xn.shape
    tm = _row_tile(m, 256)
    shift_rows = xprev is None
    if shift_rows:
        assert seq_len % tm == 0
        n_tiles = m // tm
        tails = xn.reshape(n_tiles, tm, d)[:, tm - 1]
        prev = jnp.concatenate([jnp.zeros((1, d), F32), tails[:-1]], axis=0)
        starts = (jnp.arange(n_tiles) * tm) % seq_len == 0
        xprev = jnp.where(starts[:, None], 0.0, prev).reshape(n_tiles, 1, d)
    vres = v_first is not None
    vf = v_first if vres else xn
    w1, w2 = _lora(p["w1"], p["w2"])
    a1, a2 = _lora(p["a1"], p["a2"])
    g1, g2 = _lora(p["g1"], p["g2"])
    if vres:
        v1, v2 = _lora(p["v1"], p["v2"])
        v0 = p["v0"].reshape(1, d)
    else:
        v1, v2, v0 = a1, a2, p["a0"].reshape(1, d)
    tok = pl.BlockSpec((tm, d), lambda i: (i, 0))

    def full(a):
        return pl.BlockSpec(a.shape, lambda i: (0,) * a.ndim)

    wr, wk, wv = (p["w_rkv"][c].astype(BF16) for c in range(3))
    consts = [p["mu"], wr, wk, wv, p["w0"].reshape(1, d), w1, w2, p["a0"].reshape(1, d), a1, a2,
              g1, g2, v0, v1, v2, p["k_k"].reshape(1, d), p["k_a"].reshape(1, d)]
    return pl.pallas_call(
        functools.partial(_rwkv_pre_body, vres, shift_rows),
        grid=(m // tm,),
        in_specs=[tok, pl.BlockSpec((None, 1, d), lambda i: (i, 0, 0)) if shift_rows else tok, tok]
        + [full(c) for c in consts],
        out_specs=[tok] * 7,
        out_shape=[jax.ShapeDtypeStruct((m, d), F32)] * 7,
        compiler_params=_cparams("parallel"),
        name="rwkv_pre",
    )(xn, xprev, vf, *consts)


def _split(x):
    hi = x.astype(BF16)
    lo = (x - hi.astype(F32)).astype(BF16)
    return hi, lo


_BATCH_DIMS = {"nn": ((2,), (1,)), "nt": ((2,), (2,)), "tn": ((1,), (1,))}


def _bmm(kind, a, b, passes):
    dn = (_BATCH_DIMS[kind], ((0,), (0,)))

    def dg(x, y):
        return lax.dot_general(x, y, dn, preferred_element_type=F32)

    if passes == 1:
        return dg(a.astype(BF16), b.astype(BF16))
    ah, al = _split(a)
    bh, bl = _split(b)
    return dg(ah, bh) + (dg(ah, bl) + dg(al, bh))


def _mm_exact_lhs(a01, b):
    b1 = b.astype(BF16)
    r1 = b - b1.astype(F32)
    b2 = r1.astype(BF16)
    b3 = (r1 - b2.astype(F32)).astype(BF16)
    return _dot(a01, b1) + (_dot(a01, b2) + _dot(a01, b3))


def _scan_body(L, n_heads, r_ref, lw_ref, k_ref, v_ref, kkr_ref, a_ref, rk_ref, lnw_ref, lnb_ref,
               s0_ref, y_ref, st_ref, s_sc):
    ci = pl.program_id(1)
    N = RWKV_HEAD
    HB = min(SCAN_HEADS, n_heads)

    @pl.when(ci == 0)
    def _():
        s_sc[...] = s0_ref[...]

    row = lax.broadcasted_iota(jnp.int32, (1, L, L), 1)
    col = lax.broadcasted_iota(jnp.int32, (1, L, L), 2)
    tri_incl = row >= col
    tri_strict = row > col
    same_sub = (row // SCAN_SUB) == (col // SCAN_SUB)
    ltri = tri_incl[0].astype(BF16)
    eye = (row == col).astype(F32)
    eye_n = (lax.broadcasted_iota(jnp.int32, (1, N, N), 1)
             == lax.broadcasted_iota(jnp.int32, (1, N, N), 2)).astype(F32)

    lw = lw_ref[...]
    c = _mm_exact_lhs(ltri, lw)
    ec = jnp.exp(c)
    eci = jnp.exp(-c)
    ecp = jnp.exp(c - lw)
    r = r_ref[...]
    k = k_ref[...]
    kt_all = k * eci
    rt_all = r * ec
    be_all = a_ref[...] * eci
    rkr_all = r * k * rk_ref[...]

    for g in range(n_heads // HB):
        def heads(x):
            return jnp.stack([x[:, (g * HB + i) * N:(g * HB + i + 1) * N] for i in range(HB)])

        kkr = heads(kkr_ref[...])
        kk = kkr * lax.rsqrt(jnp.maximum(jnp.sum(kkr * kkr, axis=-1, keepdims=True), 1e-24))
        at = -kk * heads(ecp)
        bt = kk * heads(be_all)
        kt = heads(kt_all)
        rt = heads(rt_all)
        v = heads(v_ref[...])
        ab = jnp.concatenate([at, rt], axis=1)
        gram_b = _bmm("nt", ab, bt, P_GRAM)
        gram_k = _bmm("nt", ab, kt, P_GRAM)
        nab = jnp.where(tri_strict, gram_b[:, :L], 0.0)
        aak = jnp.where(tri_strict, gram_k[:, :L], 0.0)
        gb = jnp.where(tri_incl, gram_b[:, L:], 0.0)
        gk = jnp.where(tri_incl, gram_k[:, L:], 0.0)
        nd = jnp.where(same_sub, nab, 0.0)
        noff = nab - nd
        nd2 = _bmm("nn", nd, nd, P_SOLVE)
        nd4 = _bmm("nn", nd2, nd2, P_SOLVE)
        nd8 = _bmm("nn", nd4, nd4, P_SOLVE)
        td = eye + nd
        td = td + _bmm("nn", td, nd2, P_SOLVE)
        td = td + _bmm("nn", td, nd4, P_SOLVE)
        td = td + _bmm("nn", td, nd8, P_SOLVE)
        rhs = jnp.concatenate([at, _bmm("nn", aak, v, P_SOLVE)], axis=2)
        wz = _bmm("nn", td, rhs, P_SOLVE)
        for _ in range(L // SCAN_SUB - 1):
            wz = _bmm("nn", td, rhs + _bmm("nn", noff, wz, P_SOLVE), P_SOLVE)
        gwz = _bmm("nn", gb, wz, P_OUT)
        rp = rt + gwz[:, :, :N]
        y0 = gwz[:, :, N:] + _bmm("nn", gk, v, P_OUT)
        ecl = heads(ec[L - 1:L, :])
        wzb = _bmm("tn", wz, bt, P_STATE)
        pm = (eye_n + wzb[:, :N]) * ecl
        qm = (wzb[:, N:] + _bmm("tn", v, kt, P_STATE)) * ecl
        s_prev = s_sc[g * HB:(g + 1) * HB]
        y = _bmm("nt", rp, s_prev, P_OUT) + y0
        s_sc[g * HB:(g + 1) * HB] = _bmm("nn", s_prev, pm, P_STATE) + qm
        mean = jnp.mean(y, axis=-1, keepdims=True)
        yc = y - mean
        var = jnp.mean(yc * yc, axis=-1, keepdims=True)
        yn = yc * lax.rsqrt(var + GN_EPS) * heads(lnw_ref[...]) + heads(lnb_ref[...])
        out = yn + jnp.sum(heads(rkr_all), axis=-1, keepdims=True) * v
        y_ref[:, g * HB * N:(g + 1) * HB * N] = jnp.concatenate([out[i] for i in range(HB)], axis=1)

    @pl.when(ci == pl.num_programs(1) - 1)
    def _():
        st_ref[...] = s_sc[...]


def _rwkv_scan(r, lw, k, v, kkr, a, r_k, lnx_w, lnx_b, s0, L):
    b, t, d = r.shape
    nh = d // RWKV_HEAD
    tok = pl.BlockSpec((None, L, d), lambda i, c: (i, c, 0))
    vec = pl.BlockSpec((1, d), lambda i, c: (0, 0))
    st = pl.BlockSpec((None, nh, RWKV_HEAD, RWKV_HEAD), lambda i, c: (i, 0, 0, 0))
    return pl.pallas_call(
        functools.partial(_scan_body, L, nh),
        grid=(b, t // L),
        in_specs=[tok] * 6 + [vec] * 3 + [st],
        out_specs=[tok, st],
        out_shape=[jax.ShapeDtypeStruct((b, t, d), F32),
                   jax.ShapeDtypeStruct((b, nh, RWKV_HEAD, RWKV_HEAD), F32)],
        scratch_shapes=[pltpu.VMEM((nh, RWKV_HEAD, RWKV_HEAD), F32)],
        compiler_params=_cparams("parallel", "arbitrary"),
        name="rwkv_scan",
    )(r, lw, k, v, kkr, a, r_k.reshape(1, d), lnx_w.reshape(1, d), lnx_b.reshape(1, d), s0)


def _compress_body(n_pages, nc, pt_ref, *refs):
    pages = refs[:n_pages]
    pos_ref, w1p_ref, w1f_ref, w2_ref, kc_ref, vc_ref, ch_sc = refs[n_pages:]
    rows = pages[0].shape[0]
    per_page = rows // CMP_STRIDE
    lane_blocks = 2 * KV_W // 128
    dst = lax.broadcasted_iota(jnp.int32, (rows, rows), 0)
    src = lax.broadcasted_iota(jnp.int32, (rows, rows), 1)
    perm = (src == (dst % per_page) * CMP_STRIDE + dst // per_page).astype(BF16)
    for p in range(n_pages):
        y = _dot(perm, pages[p][...].astype(BF16))
        for lb in range(lane_blocks):
            for rr in range(CMP_STRIDE):
                ch_sc[lb, p * per_page:(p + 1) * per_page, rr * 128:(rr + 1) * 128] = (
                    y[rr * per_page:(rr + 1) * per_page, lb * 128:(lb + 1) * 128])

    last = lax.broadcasted_iota(jnp.int32, (nc, 1), 0) == nc - 1
    for t, o_ref in enumerate((kc_ref, vc_ref)):
        posb = _dot(jnp.broadcast_to(pos_ref[t], (8, pos_ref.shape[2])).astype(BF16), w1f_ref[t])[0:1]
        for pair in range(N_KV // 2):
            ch = ch_sc[t * (N_KV // 2) + pair].astype(BF16)
            for j in range(2):
                hd = 2 * pair + j
                hh = _dot(ch, w1p_ref[t, j])
                lo = hh[:, :CMP_HID]
                hi = hh[:, CMP_HID:]
                hi = jnp.where(last, 0.0, jnp.concatenate([hi[1:], hi[:1]], axis=0))
                hid = jax.nn.gelu(lo + hi + posb)
                o_ref[:, hd * HEAD_DIM:(hd + 1) * HEAD_DIM] = _dot(hid.astype(BF16), w2_ref[t]).astype(BF16)


def _compress(src, page_table, pos, w1, w2):
    b, n_pages = page_table.shape
    rows = src.shape[1]
    nc = n_pages * rows // CMP_STRIDE
    half = CMP_STRIDE * HEAD_DIM
    w1c = jnp.concatenate([w1[:, :half], w1[:, half:]], axis=2).reshape(2, CMP_STRIDE, HEAD_DIM, 2 * CMP_HID)
    zeros = jnp.zeros_like(w1c)
    w1p = jnp.stack([jnp.concatenate([w1c, zeros], axis=2), jnp.concatenate([zeros, w1c], axis=2)], axis=1)
    w1p = w1p.reshape(2, 2, CMP_STRIDE * 128, 2 * CMP_HID).astype(BF16)
    consts = [pos.reshape(2, 1, 2 * half), w1p, w1.astype(BF16), w2.astype(BF16)]
    page_specs = [pl.BlockSpec((None, rows, 2 * KV_W), functools.partial(lambda p, i, pt: (pt[i, p], 0, 0), p))
                  for p in range(n_pages)]
    out = pl.BlockSpec((None, nc, KV_W), lambda i, pt: (i, 0, 0))
    grid_spec = pltpu.PrefetchScalarGridSpec(
        num_scalar_prefetch=1,
        grid=(b,),
        in_specs=page_specs + [pl.BlockSpec(c.shape, functools.partial(lambda n, i, pt: (0,) * n, c.ndim))
                               for c in consts],
        out_specs=[out, out],
        scratch_shapes=[pltpu.VMEM((2 * KV_W // 128, nc, CMP_STRIDE * 128), F32)],
    )
    return pl.pallas_call(
        functools.partial(_compress_body, n_pages, nc),
        grid_spec=grid_spec,
        out_shape=[jax.ShapeDtypeStruct((b, nc, KV_W), BF16)] * 2,
        compiler_params=_cparams("parallel"),
        name="nsa_compress",
    )(page_table, *([src] * n_pages), *consts)


def _query_rows(q, nq):
    pieces = []
    for kh in range(N_KV):
        for g in range(GROUP):
            hd = kh * GROUP + g
            x = q[:, hd * HEAD_DIM:(hd + 1) * HEAD_DIM] * (HEAD_DIM ** -0.5)
            parts = []
            if kh:
                parts.append(jnp.zeros((nq, kh * HEAD_DIM), F32))
            parts.append(x)
            if kh < N_KV - 1:
                parts.append(jnp.zeros((nq, (N_KV - 1 - kh) * HEAD_DIM), F32))
            pieces.append(jnp.concatenate(parts, axis=1))
    return jnp.concatenate(pieces, axis=0).astype(BF16)


def _row_meta(nq, t0):
    R = N_HEADS * nq
    rho = lax.broadcasted_iota(jnp.int32, (R, 1), 0)
    head = rho // nq
    slope = jnp.exp2(-8.0 * (head + 1).astype(F32) / N_HEADS)
    return slope, t0 + rho % nq


def _softmax_tiles(tiles):
    m = None
    for s, ok, _ in tiles:
        mt = jnp.max(jnp.where(ok, s, NEG_INF), axis=-1, keepdims=True)
        m = mt if m is None else jnp.maximum(m, mt)
    l = 0.0
    es = []
    for s, ok, _ in tiles:
        e = jnp.exp(jnp.where(ok, s, NEG_INF) - m)
        es.append(e)
        l = l + jnp.sum(e, axis=-1, keepdims=True)
    inv = 1.0 / l
    acc = 0.0
    ps = []
    for e, (s, ok, vv) in zip(es, tiles):
        p = jnp.where(ok, e * inv, 0.0)
        ps.append(p)
        if vv is not None:
            acc = acc + _dot(p.astype(BF16), vv)
    return acc, ps


def _select_blocks(p_c, nq, nc, ns, ns_pad, t_q):
    cstart = lax.broadcasted_iota(jnp.int32, (nc, ns_pad), 0) * CMP_STRIDE
    blk = lax.broadcasted_iota(jnp.int32, (nc, ns_pad), 1)
    overlap = ((cstart < (blk + 1) * SEL_BLOCK) & (cstart + CMP_LEN > blk * SEL_BLOCK)).astype(BF16)
    imps = []
    for kh in range(N_KV):
        acc = 0.0
        for g in range(GROUP):
            r0 = (kh * GROUP + g) * nq
            acc = acc + _dot(p_c[r0:r0 + nq].astype(BF16), overlap)
        imps.append(acc)
    imp = jnp.concatenate(imps, axis=0)
    lane = lax.broadcasted_iota(jnp.int32, imp.shape, 1)
    cur = t_q // SEL_BLOCK
    forced = (lane == 0) | (lane == cur) | (lane == cur - 1)
    imp = jnp.where(lane <= cur, jnp.where(forced, FORCED_SCORE, imp), -1.0)
    work = jnp.where(lane < ns, imp, REMOVED)
    sel = jnp.zeros(imp.shape, F32)
    for _ in range(min(N_SEL, ns)):
        mx = jnp.max(work, axis=-1, keepdims=True)
        first = jnp.min(jnp.where(work == mx, lane, ns_pad), axis=-1, keepdims=True)
        hit = lane == first
        sel = jnp.where(hit, 1.0, sel)
        work = jnp.where(hit, REMOVED, work)
    rows = []
    for kh in range(N_KV):
        rows += [sel[kh * nq:(kh + 1) * nq]] * GROUP
    return jnp.concatenate(rows, axis=0)


def _select_blocks_shared(p_c, nq, nc, ns, ns_pad, cur, imp_sc):
    blk = lax.broadcasted_iota(jnp.int32, (ns_pad, nc), 0)
    cstart = lax.broadcasted_iota(jnp.int32, (ns_pad, nc), 1) * CMP_STRIDE
    overlap_t = ((cstart < (blk + 1) * SEL_BLOCK) & (cstart + CMP_LEN > blk * SEL_BLOCK)).astype(BF16)
    cols = []
    for kh in range(N_KV):
        acc = 0.0
        for g in range(GROUP):
            r0 = (kh * GROUP + g) * nq
            acc = acc + _dot_nt(overlap_t, p_c[r0:r0 + nq].astype(BF16))
        cols.append(acc)
    imp = jnp.concatenate(cols, axis=1)
    b = lax.broadcasted_iota(jnp.int32, (ns_pad, 1), 0)
    forced = (b == 0) | (b == cur) | (b == cur - 1)
    imp = jnp.where(b <= cur, jnp.where(forced, FORCED_SCORE, imp), REMOVED)
    imp_sc[...] = imp

    def count(s2, cnt):
        other = imp_sc[pl.ds(s2, 1), :]
        ahead = (other > imp) | ((other == imp) & (b > s2))
        return cnt + jnp.where(ahead, 1.0, 0.0)

    cnt = lax.fori_loop(0, cur + 1, count, jnp.zeros(imp.shape, F32))
    sel_t = jnp.where((cnt < min(N_SEL, ns)) & (b <= cur), 1.0, 0.0).astype(BF16)
    eye = (lax.broadcasted_iota(jnp.int32, (ns_pad, ns_pad), 0)
           == lax.broadcasted_iota(jnp.int32, (ns_pad, ns_pad), 1)).astype(BF16)
    rows = []
    for kh in range(N_KV):
        rows += [_dot_tn(sel_t[:, kh * nq:(kh + 1) * nq], eye)] * GROUP
    return jnp.concatenate(rows, axis=0)


def _expand_sel(sel_rows, ns_pad, key0, n_keys):
    blk = lax.broadcasted_iota(jnp.int32, (ns_pad, n_keys), 0)
    kb = (key0 + lax.broadcasted_iota(jnp.int32, (ns_pad, n_keys), 1)) // SEL_BLOCK
    return _dot(sel_rows.astype(BF16), (blk == kb).astype(BF16)) > 0.5


def _compressed_branch(qrows, kc, vc, slope, t_row, nc):
    pos_end = lax.broadcasted_iota(jnp.int32, (1, nc), 1) * CMP_STRIDE + (CMP_LEN - 1)
    d_c = t_row - pos_end
    ok_c = d_c >= 0
    s_c = _dot_nt(qrows, kc) - slope * d_c.astype(F32)
    acc, ps = _softmax_tiles([(s_c, ok_c, vc)])
    return acc, ps[0]


def _combine(gates, accs, nq):
    outs = []
    for kh in range(N_KV):
        for g in range(GROUP):
            hd = kh * GROUP + g
            r0 = hd * nq
            o = 0.0
            for c in range(3):
                gcol = gates[:, 3 * hd + c:3 * hd + c + 1]
                o = o + gcol * accs[c][r0:r0 + nq, kh * HEAD_DIM:(kh + 1) * HEAD_DIM]
            outs.append(o)
    return jnp.concatenate(outs, axis=1)


F_SEL, F_DIAG, F_OLD, F_POS = 0, SEL_BLOCK, 2 * SEL_BLOCK, 3 * SEL_BLOCK
MASK_BIG = 1e30


def _key_features(t):
    p = jnp.arange(t, dtype=jnp.int32)[:, None]
    c = jnp.arange(SEL_BLOCK, dtype=jnp.int32)[None, :]
    blk, off = p // SEL_BLOCK, p % SEL_BLOCK
    f_sel = jnp.where(blk == c, MASK_BIG, 0.0)
    f_diag = jnp.where(off > c, MASK_BIG, 0.0)
    f_old = jnp.where(off < c, MASK_BIG, 0.0)
    pair = jnp.where(c % 2 == 0, blk * SEL_BLOCK, off).astype(F32)
    f_pos = jnp.where(c < 6, pair, 0.0)
    return jnp.concatenate([f_sel, f_diag, f_old, f_pos], axis=1).astype(BF16)


def _attn_prompt_body(T, nc, ns, q_ref, gate_ref, kc_ref, vc_ref, ks_ref, vs_ref, kw_ref, vw_ref, kf_ref,
                      o_ref, m_sc, l_sc, acc_sc, s_sc, imp_sc):
    nq = q_ref.shape[0]
    R = N_HEADS * nq
    i = pl.program_id(1)
    t0 = pl.multiple_of(i * nq, nq)
    qrows = _query_rows(q_ref[...], nq)
    slope, t_row = _row_meta(nq, t0)

    acc_c, p_c = _compressed_branch(qrows, kc_ref[...], vc_ref[...], slope, t_row, nc)
    sel_rows = _select_blocks_shared(p_c, nq, nc, ns, SEL_BLOCK, i, imp_sc)

    lane = lax.broadcasted_iota(jnp.int32, (R, SEL_BLOCK), 1)
    q_loc = lax.broadcasted_iota(jnp.int32, (R, 1), 0) % nq
    s1 = slope.astype(BF16).astype(F32)
    s2 = (slope - s1).astype(BF16).astype(F32)
    s3 = slope - s1 - s2
    f_slope = jnp.where(lane < 2, s1, jnp.where(lane < 4, s2, jnp.where(lane < 6, s3, 0.0)))
    neg_hot = jnp.where(lane == q_loc, -1.0, 0.0)
    zero = jnp.zeros((R, SEL_BLOCK), F32)
    earlier = jnp.where(lane < i, sel_rows, 0.0) - 1.0

    def feat(a, b, c):
        return jnp.concatenate([a, b, c, f_slope], axis=1).astype(BF16)

    qf_main = feat(earlier, zero, zero)
    qf_diag = feat(zero, neg_hot, zero)

    fd = kf_ref[pl.ds(t0, nq), :]
    s_d = _dot_nt(qrows, ks_ref[pl.ds(t0, nq), :]) + _dot_nt(qf_diag, fd)
    m0 = jnp.max(s_d, axis=-1, keepdims=True)
    p0 = jnp.exp(s_d - m0)
    m_sc[...] = m0
    l_sc[...] = jnp.sum(p0, axis=-1, keepdims=True)
    acc_sc[...] = _dot(p0.astype(BF16), vs_ref[pl.ds(t0, nq), :])
    kt = min(KEY_TILE, T)

    def scores(j):
        k0 = pl.multiple_of(jnp.minimum(j, T // kt - 1) * kt, kt)
        return _dot_nt(qrows, ks_ref[pl.ds(k0, kt), :]) + _dot_nt(qf_main, kf_ref[pl.ds(k0, kt), :])

    s_sc[0] = scores(0)

    def tile(j, carry):
        slot = j % 2
        s = s_sc[slot]
        s_sc[1 - slot] = scores(j + 1)
        k0 = pl.multiple_of(j * kt, kt)
        m_old = m_sc[...]
        m_new = jnp.maximum(m_old, jnp.max(s, axis=-1, keepdims=True))
        p = jnp.exp(s - m_new)
        scale = jnp.exp(m_old - m_new)
        l_sc[...] = scale * l_sc[...] + jnp.sum(p, axis=-1, keepdims=True)
        acc_sc[...] = scale * acc_sc[...] + _dot(p.astype(BF16), vs_ref[pl.ds(k0, kt), :])
        m_sc[...] = m_new
        return carry

    lax.fori_loop(0, (t0 + kt - 1) // kt, tile, 0)
    acc_s = acc_sc[...] * (1.0 / l_sc[...])

    nw = min(WINDOW + nq, T)

    def window_masked(_):
        w0 = pl.multiple_of(jnp.clip(t0 - WINDOW, 0, T - nw), nq)
        posw = w0 + lax.broadcasted_iota(jnp.int32, (1, nw), 1)
        d_w = t_row - posw
        ok_w = (d_w >= 0) & (d_w <= WINDOW)
        s_w = _dot_nt(qrows, kw_ref[pl.ds(w0, nw), :]) - slope * d_w.astype(F32)
        return _softmax_tiles([(s_w, ok_w, vw_ref[pl.ds(w0, nw), :])])[0]

    def window_full(_):
        w0 = pl.multiple_of(t0 - WINDOW, nq)
        rw = lax.broadcasted_iota(jnp.int32, (nw, KV_W), 0)
        cw = lax.broadcasted_iota(jnp.int32, (nw, KV_W), 1)
        keep = ((cw >= F_POS) | ((rw < nq) & (cw >= F_OLD) & (cw < F_POS))
                | ((rw >= WINDOW) & (cw >= F_DIAG) & (cw < F_OLD)))
        kfw = jnp.where(keep, kf_ref[pl.ds(w0, nw), :].astype(F32), 0.0).astype(BF16)
        s_w = _dot_nt(qrows, kw_ref[pl.ds(w0, nw), :]) + _dot_nt(feat(zero, neg_hot, neg_hot), kfw)
        e = jnp.exp(s_w - jnp.max(s_w, axis=-1, keepdims=True))
        acc = _dot(e.astype(BF16), vw_ref[pl.ds(w0, nw), :])
        return acc * (1.0 / jnp.sum(e, axis=-1, keepdims=True))

    if T >= 2 * WINDOW:
        acc_w = lax.cond(i >= WINDOW // nq, window_full, window_masked, 0)
    else:
        acc_w = window_masked(0)

    gates = jax.nn.sigmoid(gate_ref[...])
    o_ref[...] = _combine(gates, (acc_c, acc_s, acc_w), nq)


def _attn_prompt(z, kvb, kc, vc, nq):
    b, t, _ = z.shape
    nqd = N_HEADS * HEAD_DIM
    nc = kc.shape[1]
    ns = t // SEL_BLOCK
    R = N_HEADS * nq
    assert nq == SEL_BLOCK and ns <= SEL_BLOCK and KV_W == 4 * SEL_BLOCK
    kvspec = [pl.BlockSpec((None, t, KV_W), functools.partial(lambda c, i, j: (i, 0, c), c))
              for c in (2, 3, 4, 5)]
    return pl.pallas_call(
        functools.partial(_attn_prompt_body, t, nc, ns),
        grid=(b, t // nq),
        in_specs=[
            pl.BlockSpec((None, nq, nqd), lambda i, j: (i, j, 0)),
            pl.BlockSpec((None, nq, 128), lambda i, j: (i, j, nqd // 128)),
            pl.BlockSpec((None, nc, KV_W), lambda i, j: (i, 0, 0)),
            pl.BlockSpec((None, nc, KV_W), lambda i, j: (i, 0, 0)),
        ] + kvspec + [pl.BlockSpec((t, KV_W), lambda i, j: (0, 0))],
        out_specs=pl.BlockSpec((None, nq, nqd), lambda i, j: (i, j, 0)),
        out_shape=jax.ShapeDtypeStruct((b, t, nqd), F32),
        scratch_shapes=[pltpu.VMEM((R, 1), F32), pltpu.VMEM((R, 1), F32), pltpu.VMEM((R, KV_W), F32),
                        pltpu.VMEM((2, R, min(KEY_TILE, t)), F32), pltpu.VMEM((SEL_BLOCK, N_KV * nq), F32)],
        compiler_params=_cparams("parallel", "arbitrary"),
        name="nsa_attn_prompt",
    )(z, z, kc, vc, kvb, kvb, kvb, kvb, _key_features(t))


def _attn_sample_body(n_pages, page, past, tq, nc, ns, ns_pad, pt_ref, q_ref, gate_ref, kc_ref, vc_ref,
                      *rest):
    kpages = rest[:n_pages]
    vpages = rest[n_pages:2 * n_pages]
    kn_ref, vn_ref, kwn_ref, vwn_ref, kwp_ref, vwp_ref, o_ref = rest[2 * n_pages:]
    nq = q_ref.shape[0]
    nn = kn_ref.shape[0]
    qrows = _query_rows(q_ref[...], nq)
    slope, t_row = _row_meta(nq, past)
    t_row = jnp.minimum(t_row, past + tq - 1)

    acc_c, p_c = _compressed_branch(qrows, kc_ref[...], vc_ref[...], slope, t_row, nc)
    t_q = jnp.minimum(past + lax.broadcasted_iota(jnp.int32, (N_KV * nq, 1), 0) % nq, past + tq - 1)
    sel_rows = _select_blocks(p_c, nq, nc, ns, ns_pad, t_q)

    def new_tile(k_ref, v_ref, sel):
        pos = past + lax.broadcasted_iota(jnp.int32, (1, nn), 1)
        d = t_row - pos
        ok = d >= 0
        if sel:
            ok = ok & _expand_sel(sel_rows, ns_pad, past, nn)
        s = _dot_nt(qrows, k_ref[...]) - slope * d.astype(F32)
        return s, ok, v_ref[...]

    tiles = []
    for p in range(n_pages):
        pos = p * page + lax.broadcasted_iota(jnp.int32, (1, page), 1)
        d = t_row - pos
        ok = (d >= 0) & _expand_sel(sel_rows, ns_pad, p * page, page)
        s = _dot_nt(qrows, kpages[p][...].astype(BF16)) - slope * d.astype(F32)
        tiles.append((s, ok, vpages[p][...].astype(BF16)))
    tiles.append(new_tile(kn_ref, vn_ref, True))
    acc_s, _ = _softmax_tiles(tiles)

    nwp = kwp_ref.shape[0]
    w_pos0 = past - nwp
    posw = w_pos0 + lax.broadcasted_iota(jnp.int32, (1, nwp), 1)
    d_w = t_row - posw
    ok_w = (d_w >= 0) & (d_w <= WINDOW)
    s_w = _dot_nt(qrows, kwp_ref[...].astype(BF16)) - slope * d_w.astype(F32)
    sn, okn, vn = new_tile(kwn_ref, vwn_ref, False)
    dn = t_row - (past + lax.broadcasted_iota(jnp.int32, (1, nn), 1))
    acc_w, _ = _softmax_tiles([(s_w, ok_w, vwp_ref[...].astype(BF16)), (sn, okn & (dn <= WINDOW), vn)])

    gates = jax.nn.sigmoid(gate_ref[...])
    o_ref[...] = _combine(gates, (acc_c, acc_s, acc_w), nq)


def _attn_sample(z, kvb, kc, vc, cache3, win3, page_table, tq):
    b, nq, _ = z.shape
    nqd = N_HEADS * HEAD_DIM
    n_pages = page_table.shape[1]
    page = cache3.shape[1]
    past = n_pages * page
    nn = kvb.shape[1]
    nc = kc.shape[1]
    ns = -(-(past + tq) // SEL_BLOCK)
    ns_pad = -(-ns // 128) * 128
    nwp = win3.shape[1]
    assert past % SEL_BLOCK == 0 and nn <= SEL_BLOCK and nwp == WINDOW
    kp = [pl.BlockSpec((None, page, KV_W), functools.partial(lambda p, i, pt: (pt[i, p], 0, 2), p))
          for p in range(n_pages)]
    vp = [pl.BlockSpec((None, page, KV_W), functools.partial(lambda p, i, pt: (pt[i, p], 0, 3), p))
          for p in range(n_pages)]
    newspec = [pl.BlockSpec((None, nn, KV_W), functools.partial(lambda c, i, pt: (i, 0, c), c))
               for c in (2, 3, 4, 5)]
    grid_spec = pltpu.PrefetchScalarGridSpec(
        num_scalar_prefetch=1,
        grid=(b,),
        in_specs=[
            pl.BlockSpec((None, nq, nqd), lambda i, pt: (i, 0, 0)),
            pl.BlockSpec((None, nq, 128), lambda i, pt: (i, 0, nqd // 128)),
            pl.BlockSpec((None, nc, KV_W), lambda i, pt: (i, 0, 0)),
            pl.BlockSpec((None, nc, KV_W), lambda i, pt: (i, 0, 0)),
        ] + kp + vp + newspec + [
            pl.BlockSpec((None, nwp, KV_W), lambda i, pt: (i, 0, 0)),
            pl.BlockSpec((None, nwp, KV_W), lambda i, pt: (i, 0, 1)),
        ],
        out_specs=pl.BlockSpec((None, nq, nqd), lambda i, pt: (i, 0, 0)),
    )
    return pl.pallas_call(
        functools.partial(_attn_sample_body, n_pages, page, past, tq, nc, ns, ns_pad),
        grid_spec=grid_spec,
        out_shape=jax.ShapeDtypeStruct((b, nq, nqd), F32),
        compiler_params=_cparams("parallel"),
        name="nsa_attn_sample",
    )(page_table, z, z, kc, vc, *([cache3] * (2 * n_pages)), kvb, kvb, kvb, kvb, win3, win3)


def kernel(x_prompt, x_sample, cache_cs, cache_win, state_shift, state_wkv, page_table, norm_g, ffn_w_in, ffn_w_out, rw_mu, rw_w_rkv, rw_w0, rw_w1, rw_w2, rw_a0, rw_a1, rw_a2, rw_g1, rw_g2, rw_k_k, rw_k_a, rw_r_k, rw_lnx_w, rw_lnx_b, rw_w_o, rw_v0, rw_v1, rw_v2, kv_norm_g, kv_w, cmp_pos, cmp_w1, cmp_w2, b_w_in, b_w_o, final_g):
    bp, tp, d = x_prompt.shape
    bs, ts, _ = x_sample.shape
    depth = norm_g.shape[0]
    n_a = rw_mu.shape[0]
    nh = d // RWKV_HEAD
    mp, ms = bp * tp, bs * ts
    n_pool, page = cache_cs.shape[:2]
    n_pages = page_table.shape[1]
    past = n_pages * page
    nqd = N_HEADS * HEAD_DIM
    ts_pad = 8
    assert tp % SCAN_CHUNK == 0 and tp % page == 0 and tp % SEL_BLOCK == 0
    nq_p = max(1, min(64, tp, 256 // bp))
    assert nq_p == SEL_BLOCK and ts <= ts_pad
    assert (past + ts - CMP_LEN) // CMP_STRIDE * CMP_STRIDE + CMP_LEN <= past

    cache3 = cache_cs.reshape(n_pool, page, 4 * KV_W)
    win3 = cache_win.reshape(bs, cache_win.shape[1], 2 * KV_W)

    def pad_tokens(t):
        return jnp.pad(t.reshape(bs, ts, -1), ((0, 0), (0, ts_pad - ts), (0, 0)))

    h = [x_prompt.reshape(mp, d), x_sample.reshape(ms, d)]
    shifts = [[], []]
    states = [[], []]
    v_first = [None, None]
    kv = kvb = kc = vc = [None, None]
    y = [None, None]
    for l in range(depth):
        g = norm_g[l]
        w_in0, w_out0 = ffn_w_in[l, 0].astype(BF16), ffn_w_out[l, 0].astype(BF16)
        w_in1, w_out1 = ffn_w_in[l, 1].astype(BF16), ffn_w_out[l, 1].astype(BF16)
        last = l == depth - 1
        if l < n_a:
            p = dict(mu=rw_mu[l], w_rkv=rw_w_rkv[l], w0=rw_w0[l], w1=rw_w1[l], w2=rw_w2[l], a0=rw_a0[l],
                     a1=rw_a1[l], a2=rw_a2[l], g1=rw_g1[l], g2=rw_g2[l], k_k=rw_k_k[l], k_a=rw_k_a[l])
            if l > 0:
                p.update(v0=rw_v0[l - 1], v1=rw_v1[l - 1], v2=rw_v2[l - 1])
            w_o = rw_w_o[l].astype(BF16)
        else:
            w_q = _pad_cols(b_w_in[l - n_a], nqd + 128).astype(BF16)
            w_o = b_w_o[l - n_a].astype(BF16)
        for s in range(2):
            h[s], xn = _ffn(h[s], g[0], w_in0, w_out0, g_post=g[1])
            if l < n_a:
                if s == 0:
                    shifts[s].append(xn.reshape(bp, tp, d)[:, -1])
                    pre = _rwkv_pre(xn, None, v_first[s], p, seq_len=tp)
                    s0, chunk = jnp.zeros((bp, nh, RWKV_HEAD, RWKV_HEAD), F32), SCAN_CHUNK
                    seqs = [t.reshape(bp, tp, d) for t in pre[:6]]
                else:
                    xs = xn.reshape(bs, ts, d)
                    shifts[s].append(xs[:, -1])
                    xprev = jnp.concatenate([state_shift[l][:, None, :], xs[:, :-1]], axis=1).reshape(ms, d)
                    pre = _rwkv_pre(xn, xprev, v_first[s], p)
                    s0, chunk = state_wkv[l], ts_pad
                    seqs = [pad_tokens(t) for t in pre[:6]]
                if l == 0:
                    v_first[s] = pre[3]
                ymix, st = _rwkv_scan(*seqs, rw_r_k[l], rw_lnx_w[l], rw_lnx_b[l], s0, chunk)
                states[s].append(st)
                ymix = ymix.reshape(mp, d) if s == 0 else ymix[:, :ts].reshape(ms, d)
                h[s] = _matmul_res(ymix, pre[6], w_o, h[s])
            else:
                z, _ = _norm_matmul(xn, None, w_q)
                if s == 0:
                    o = _attn_prompt(z.reshape(bp, tp, nqd + 128), kvb[s].reshape(bp, tp, 6 * KV_W),
                                     kc[s], vc[s], nq_p).reshape(mp, nqd)
                else:
                    o = _attn_sample(pad_tokens(z), pad_tokens(kvb[s]), kc[s], vc[s], cache3, win3,
                                     page_table, ts)[:, :ts].reshape(ms, nqd)
                h[s] = _matmul_res(o, None, w_o, h[s])
            res = _ffn(h[s], g[2], w_in1, w_out1, g_post=final_g if last else None)
            if last:
                h[s], y[s] = res
            else:
                h[s] = res
        if l == n_a - 1:
            kv_w_b = kv_w.astype(BF16)
            kv0, kvb0 = _norm_matmul(h[0], kv_norm_g, kv_w_b)
            kv1, kvb1 = _norm_matmul(h[1], kv_norm_g, kv_w_b)
            kv, kvb = [kv0, kv1], [kvb0, kvb1]
            ident = jnp.arange(mp // page, dtype=jnp.int32).reshape(bp, tp // page)
            kc0, vc0 = _compress(kv0.reshape(mp // page, page, 6 * KV_W), ident, cmp_pos, cmp_w1, cmp_w2)
            kc1, vc1 = _compress(cache3, page_table, cmp_pos, cmp_w1, cmp_w2)
            kc, vc = [kc0, kc1], [vc0, vc1]

    kv_p = kv[0].reshape(bp, tp, 6, N_KV, HEAD_DIM)
    kv_s = kv[1].reshape(bs, ts, 6, N_KV, HEAD_DIM)
    win_all = jnp.concatenate([cache_win, kv_s[:, :, 4:]], axis=1)
    s_win = win_all[:, win_all.shape[1] - cache_win.shape[1]:]
    return (y[0].reshape(bp, tp, d), y[1].reshape(bs, ts, d), jnp.stack(shifts[0]), jnp.stack(states[0]),
            kv_p[:, :, :4], kv_p[:, tp - min(WINDOW, tp):, 4:],
            jnp.stack(shifts[1]), jnp.stack(states[1]), kv_s[:, :, :4], s_win)
```

```python
import functools

import jax
import jax.numpy as jnp
from jax import lax
from jax.experimental import pallas as pl
from jax.experimental.pallas import tpu as pltpu

F32 = jnp.float32
BF16 = jnp.bfloat16

RWKV_HEAD = 64
GN_EPS = 64e-5
N_HEADS = 16
HEAD_DIM = 64
N_KV = 4
GROUP = N_HEADS // N_KV
KV_W = N_KV * HEAD_DIM
CMP_STRIDE = 16
CMP_LEN = 2 * CMP_STRIDE
CMP_HID = 2 * HEAD_DIM
SEL_BLOCK = 64
N_SEL = 16
WINDOW = 512
RMS_EPS = 1e-6
NEG_INF = -1e30
FORCED_SCORE = 1e4
REMOVED = -3e38

SCAN_CHUNK = 64
SCAN_SUB = 16
SCAN_HEADS = 16
P_GRAM, P_SOLVE, P_OUT, P_STATE = 1, 1, 1, 3
KEY_TILE = 512
VMEM_LIMIT = 56 * 1024 * 1024


def _cparams(*sem):
    return pltpu.CompilerParams(dimension_semantics=sem, vmem_limit_bytes=VMEM_LIMIT)


def _row_tile(m, cap):
    t = cap
    while t > 8 and m % t:
        t //= 2
    assert m % t == 0, (m, cap)
    return t


def _rms(x, g):
    return x * lax.rsqrt(jnp.mean(x * x, axis=-1, keepdims=True) + RMS_EPS) * g


def _dot(a, b):
    return jnp.dot(a, b, preferred_element_type=F32)


def _dot_nt(a, b):
    return lax.dot_general(a, b, (((1,), (1,)), ((), ())), preferred_element_type=F32)


def _dot_tn(a, b):
    return lax.dot_general(a, b, (((0,), (0,)), ((), ())), preferred_element_type=F32)


def _ffn_body(nf, post, h_ref, g_ref, wg_ref, wu_ref, wo_ref, gp_ref, *rest):
    if post:
        o_ref, on_ref, xn_sc, acc_sc = rest
    else:
        o_ref, xn_sc, acc_sc = rest
    f = pl.program_id(1)

    @pl.when(f == 0)
    def _():
        xn_sc[...] = _rms(h_ref[...], g_ref[...]).astype(BF16)
        acc_sc[...] = jnp.zeros_like(acc_sc)

    xn = xn_sc[...]
    gate = _dot(xn, wg_ref[...])
    up = _dot(xn, wu_ref[...])
    hid = (gate * jax.nn.sigmoid(gate) * up).astype(BF16)
    acc_sc[...] += _dot(hid, wo_ref[...])

    @pl.when(f == nf - 1)
    def _():
        hn = h_ref[...] + 0.5 * acc_sc[...]
        o_ref[...] = hn
        if post:
            on_ref[...] = _rms(hn, gp_ref[...])


def _ffn(h, g, w_in, w_out, g_post=None):
    m, d = h.shape
    dff = w_out.shape[0]
    tm = _row_tile(m, 512)
    tf = dff // 2 if (dff // 2) % 128 == 0 else dff
    nf = dff // tf
    post = g_post is not None
    gp = g_post if post else g
    out_shape = [jax.ShapeDtypeStruct((m, d), F32)]
    out_specs = [pl.BlockSpec((tm, d), lambda i, f: (i, 0))]
    if post:
        out_shape.append(jax.ShapeDtypeStruct((m, d), F32))
        out_specs.append(pl.BlockSpec((tm, d), lambda i, f: (i, 0)))
    res = pl.pallas_call(
        functools.partial(_ffn_body, nf, post),
        grid=(m // tm, nf),
        in_specs=[
            pl.BlockSpec((tm, d), lambda i, f: (i, 0)),
            pl.BlockSpec((1, d), lambda i, f: (0, 0)),
            pl.BlockSpec((d, tf), lambda i, f: (0, f)),
            pl.BlockSpec((d, tf), lambda i, f: (0, nf + f)),
            pl.BlockSpec((tf, d), lambda i, f: (f, 0)),
            pl.BlockSpec((1, d), lambda i, f: (0, 0)),
        ],
        out_specs=out_specs,
        out_shape=out_shape,
        scratch_shapes=[pltpu.VMEM((tm, d), BF16), pltpu.VMEM((tm, d), F32)],
        compiler_params=_cparams("parallel", "arbitrary"),
        name="ffn",
    )(h, g.reshape(1, d), w_in, w_in, w_out, gp.reshape(1, d))
    return res if post else res[0]


def _norm_matmul_body(norm, x_ref, g_ref, w_ref, o_ref, ob_ref):
    x = x_ref[...]
    if norm:
        x = _rms(x, g_ref[...])
    y = _dot(x.astype(BF16), w_ref[...])
    o_ref[...] = y
    ob_ref[...] = y.astype(BF16)


def _norm_matmul(x, g, w):
    m, d = x.shape
    n = w.shape[1]
    tm = _row_tile(m, 512)
    norm = g is not None
    gg = g.reshape(1, d) if norm else jnp.ones((1, d), F32)
    return pl.pallas_call(
        functools.partial(_norm_matmul_body, norm),
        grid=(m // tm,),
        in_specs=[
            pl.BlockSpec((tm, d), lambda i: (i, 0)),
            pl.BlockSpec((1, d), lambda i: (0, 0)),
            pl.BlockSpec((d, n), lambda i: (0, 0)),
        ],
        out_specs=[pl.BlockSpec((tm, n), lambda i: (i, 0)),
                   pl.BlockSpec((tm, n), lambda i: (i, 0))],
        out_shape=[jax.ShapeDtypeStruct((m, n), F32), jax.ShapeDtypeStruct((m, n), BF16)],
        compiler_params=_cparams("parallel"),
        name="norm_matmul",
    )(x, gg, w)


def _matmul_res_body(gated, x_ref, g_ref, w_ref, h_ref, o_ref):
    x = x_ref[...]
    if gated:
        x = x * g_ref[...]
    o_ref[...] = h_ref[...] + _dot(x.astype(BF16), w_ref[...])


def _matmul_res(x, gate, w, h):
    m, k = x.shape
    n = w.shape[1]
    tm = _row_tile(m, 512)
    gated = gate is not None
    gg = gate if gated else x
    return pl.pallas_call(
        functools.partial(_matmul_res_body, gated),
        grid=(m // tm,),
        in_specs=[
            pl.BlockSpec((tm, k), lambda i: (i, 0)),
            pl.BlockSpec((tm, k), lambda i: (i, 0)),
            pl.BlockSpec((k, n), lambda i: (0, 0)),
            pl.BlockSpec((tm, n), lambda i: (i, 0)),
        ],
        out_specs=pl.BlockSpec((tm, n), lambda i: (i, 0)),
        out_shape=jax.ShapeDtypeStruct((m, n), F32),
        compiler_params=_cparams("parallel"),
        name="matmul_res",
    )(x, gg, w, h)


def _softplus(x):
    return jnp.maximum(x, 0.0) + jnp.log1p(jnp.exp(-jnp.abs(x)))


def _rwkv_pre_body(vres, shift_rows, x_ref, xp_ref, vf_ref, mu_ref, wr_ref, wk_ref, wv_ref,
                   w0_ref, w1_ref, w2_ref, a0_ref, a1_ref, a2_ref, g1_ref, g2_ref,
                   v0_ref, v1_ref, v2_ref, kk_ref, ka_ref,
                   r_o, lw_o, k_o, v_o, kkr_o, a_o, g_o):
    x = x_ref[...]
    if shift_rows:
        first = lax.broadcasted_iota(jnp.int32, x.shape, 0) == 0
        xp = jnp.where(first, xp_ref[...], pltpu.roll(x, 1, axis=0))
    else:
        xp = xp_ref[...]
    xx = xp - x

    def lerp(c):
        return (x + xx * mu_ref[c:c + 1, :]).astype(BF16)

    r = _dot(lerp(0), wr_ref[...])
    k = _dot(lerp(1), wk_ref[...])
    xv = lerp(2)
    v = _dot(xv, wv_ref[...])
    w = w0_ref[...] + _dot(jnp.tanh(_dot(lerp(3), w1_ref[...])).astype(BF16), w2_ref[...])
    w = -_softplus(-w) - 0.5
    a = jax.nn.sigmoid(a0_ref[...] + _dot(_dot(lerp(4), a1_ref[...]).astype(BF16), a2_ref[...]))
    g = _dot(jax.nn.sigmoid(_dot(lerp(5), g1_ref[...])).astype(BF16), g2_ref[...])
    if vres:
        mix = jax.nn.sigmoid(v0_ref[...] + _dot(_dot(xv, v1_ref[...]).astype(BF16), v2_ref[...]))
        v = v + (vf_ref[...] - v) * mix
    r_o[...] = r
    lw_o[...] = -jnp.exp(w)
    k_o[...] = k * (1.0 + (a - 1.0) * ka_ref[...])
    v_o[...] = v
    kkr_o[...] = k * kk_ref[...]
    a_o[...] = a
    g_o[...] = g


def _pad_cols(w, n):
    return jnp.pad(w, ((0, 0), (0, n - w.shape[1])))


def _pad_rows(w, n):
    return jnp.pad(w, ((0, n - w.shape[0]), (0, 0)))


def _lora(w_down, w_up):
    r = w_down.shape[1]
    rp = -(-r // 128) * 128
    return _pad_cols(w_down, rp).astype(BF16), _pad_rows(w_up, rp).astype(BF16)


def _rwkv_pre(xn, xprev, v_first, p, seq_len=None):
    m, d = xn.shape
    tm = _row_tile(m, 256)
    shift_rows = xprev is None
    if shift_rows:
        assert seq_len % tm == 0
        n_tiles = m // tm
        tails = xn.reshape(n_tiles, tm, d)[:, tm - 1]
        prev = jnp.concatenate([jnp.zeros((1, d), F32), tails[:-1]], axis=0)
        starts = (jnp.arange(n_tiles) * tm) % seq_len == 0
        xprev = jnp.where(starts[:, None], 0.0, prev).reshape(n_tiles, 1, d)
    vres = v_first is not None
    vf = v_first if vres else xn
    w1, w2 = _lora(p["w1"], p["w2"])
    a1, a2 = _lora(p["a1"], p["a2"])
    g1, g2 = _lora(p["g1"], p["g2"])
    if vres:
        v1, v2 = _lora(p["v1"], p["v2"])
        v0 = p["v0"].reshape(1, d)
    else:
        v1, v2, v0 = a1, a2, p["a0"].reshape(1, d)
    tok = pl.BlockSpec((tm, d), lambda i: (i, 0))

    def full(a):
        return pl.BlockSpec(a.shape, lambda i: (0,) * a.ndim)

    wr, wk, wv = (p["w_rkv"][c].astype(BF16) for c in range(3))
    consts = [p["mu"], wr, wk, wv, p["w0"].reshape(1, d), w1, w2, p["a0"].reshape(1, d), a1, a2,
              g1, g2, v0, v1, v2, p["k_k"].reshape(1, d), p["k_a"].reshape(1, d)]
    return pl.pallas_call(
        functools.partial(_rwkv_pre_body, vres, shift_rows),
        grid=(m // tm,),
        in_specs=[tok, pl.BlockSpec((None, 1, d), lambda i: (i, 0, 0)) if shift_rows else tok, tok]
        + [full(c) for c in consts],
        out_specs=[tok] * 7,
        out_shape=[jax.ShapeDtypeStruct((m, d), F32)] * 7,
        compiler_params=_cparams("parallel"),
        name="rwkv_pre",
    )(xn, xprev, vf, *consts)


def _split(x):
    hi = x.astype(BF16)
    lo = (x - hi.astype(F32)).astype(BF16)
    return hi, lo


_BATCH_DIMS = {"nn": ((2,), (1,)), "nt": ((2,), (2,)), "tn": ((1,), (1,))}


def _bmm(kind, a, b, passes):
    dn = (_BATCH_DIMS[kind], ((0,), (0,)))

    def dg(x, y):
        return lax.dot_general(x, y, dn, preferred_element_type=F32)

    if passes == 1:
        return dg(a.astype(BF16), b.astype(BF16))
    ah, al = _split(a)
    bh, bl = _split(b)
    return dg(ah, bh) + (dg(ah, bl) + dg(al, bh))


def _mm_exact_lhs(a01, b):
    b1 = b.astype(BF16)
    r1 = b - b1.astype(F32)
    b2 = r1.astype(BF16)
    b3 = (r1 - b2.astype(F32)).astype(BF16)
    return _dot(a01, b1) + (_dot(a01, b2) + _dot(a01, b3))


def _scan_body(L, n_heads, r_ref, lw_ref, k_ref, v_ref, kkr_ref, a_ref, rk_ref, lnw_ref, lnb_ref,
               s0_ref, y_ref, st_ref, s_sc):
    ci = pl.program_id(1)
    N = RWKV_HEAD
    HB = min(SCAN_HEADS, n_heads)

    @pl.when(ci == 0)
    def _():
        s_sc[...] = s0_ref[...]

    row = lax.broadcasted_iota(jnp.int32, (1, L, L), 1)
    col = lax.broadcasted_iota(jnp.int32, (1, L, L), 2)
    tri_incl = row >= col
    tri_strict = row > col
    same_sub = (row // SCAN_SUB) == (col // SCAN_SUB)
    ltri = tri_incl[0].astype(BF16)
    eye = (row == col).astype(F32)
    eye_n = (lax.broadcasted_iota(jnp.int32, (1, N, N), 1)
             == lax.broadcasted_iota(jnp.int32, (1, N, N), 2)).astype(F32)

    lw = lw_ref[...]
    c = _mm_exact_lhs(ltri, lw)
    ec = jnp.exp(c)
    eci = jnp.exp(-c)
    ecp = jnp.exp(c - lw)
    r = r_ref[...]
    k = k_ref[...]
    kt_all = k * eci
    rt_all = r * ec
    be_all = a_ref[...] * eci
    rkr_all = r * k * rk_ref[...]

    for g in range(n_heads // HB):
        def heads(x):
            return jnp.stack([x[:, (g * HB + i) * N:(g * HB + i + 1) * N] for i in range(HB)])

        kkr = heads(kkr_ref[...])
        kk = kkr * lax.rsqrt(jnp.maximum(jnp.sum(kkr * kkr, axis=-1, keepdims=True), 1e-24))
        at = -kk * heads(ecp)
        bt = kk * heads(be_all)
        kt = heads(kt_all)
        rt = heads(rt_all)
        v = heads(v_ref[...])
        ab = jnp.concatenate([at, rt], axis=1)
        gram_b = _bmm("nt", ab, bt, P_GRAM)
        gram_k = _bmm("nt", ab, kt, P_GRAM)
        nab = jnp.where(tri_strict, gram_b[:, :L], 0.0)
        aak = jnp.where(tri_strict, gram_k[:, :L], 0.0)
        gb = jnp.where(tri_incl, gram_b[:, L:], 0.0)
        gk = jnp.where(tri_incl, gram_k[:, L:], 0.0)
        nd = jnp.where(same_sub, nab, 0.0)
        noff = nab - nd
        nd2 = _bmm("nn", nd, nd, P_SOLVE)
        nd4 = _bmm("nn", nd2, nd2, P_SOLVE)
        nd8 = _bmm("nn", nd4, nd4, P_SOLVE)
        td = eye + nd
        td = td + _bmm("nn", td, nd2, P_SOLVE)
        td = td + _bmm("nn", td, nd4, P_SOLVE)
        td = td + _bmm("nn", td, nd8, P_SOLVE)
        rhs = jnp.concatenate([at, _bmm("nn", aak, v, P_SOLVE)], axis=2)
        wz = _bmm("nn", td, rhs, P_SOLVE)
        for _ in range(L // SCAN_SUB - 1):
            wz = _bmm("nn", td, rhs + _bmm("nn", noff, wz, P_SOLVE), P_SOLVE)
        gwz = _bmm("nn", gb, wz, P_OUT)
        rp = rt + gwz[:, :, :N]
        y0 = gwz[:, :, N:] + _bmm("nn", gk, v, P_OUT)
        ecl = heads(ec[L - 1:L, :])
        wzb = _bmm("tn", wz, bt, P_STATE)
        pm = (eye_n + wzb[:, :N]) * ecl
        qm = (wzb[:, N:] + _bmm("tn", v, kt, P_STATE)) * ecl
        s_prev = s_sc[g * HB:(g + 1) * HB]
        y = _bmm("nt", rp, s_prev, P_OUT) + y0
        s_sc[g * HB:(g + 1) * HB] = _bmm("nn", s_prev, pm, P_STATE) + qm
        mean = jnp.mean(y, axis=-1, keepdims=True)
        yc = y - mean
        var = jnp.mean(yc * yc, axis=-1, keepdims=True)
        yn = yc * lax.rsqrt(var + GN_EPS) * heads(lnw_ref[...]) + heads(lnb_ref[...])
        out = yn + jnp.sum(heads(rkr_all), axis=-1, keepdims=True) * v
        y_ref[:, g * HB * N:(g + 1) * HB * N] = jnp.concatenate([out[i] for i in range(HB)], axis=1)

    @pl.when(ci == pl.num_programs(1) - 1)
    def _():
        st_ref[...] = s_sc[...]


def _rwkv_scan(r, lw, k, v, kkr, a, r_k, lnx_w, lnx_b, s0, L):
    b, t, d = r.shape
    nh = d // RWKV_HEAD
    tok = pl.BlockSpec((None, L, d), lambda i, c: (i, c, 0))
    vec = pl.BlockSpec((1, d), lambda i, c: (0, 0))
    st = pl.BlockSpec((None, nh, RWKV_HEAD, RWKV_HEAD), lambda i, c: (i, 0, 0, 0))
    return pl.pallas_call(
        functools.partial(_scan_body, L, nh),
        grid=(b, t // L),
        in_specs=[tok] * 6 + [vec] * 3 + [st],
        out_specs=[tok, st],
        out_shape=[jax.ShapeDtypeStruct((b, t, d), F32),
                   jax.ShapeDtypeStruct((b, nh, RWKV_HEAD, RWKV_HEAD), F32)],
        scratch_shapes=[pltpu.VMEM((nh, RWKV_HEAD, RWKV_HEAD), F32)],
        compiler_params=_cparams("parallel", "arbitrary"),
        name="rwkv_scan",
    )(r, lw, k, v, kkr, a, r_k.reshape(1, d), lnx_w.reshape(1, d), lnx_b.reshape(1, d), s0)


def _compress_body(n_pages, nc, pt_ref, *refs):
    pages = refs[:n_pages]
    pos_ref, w1p_ref, w1f_ref, w2_ref, kc_ref, vc_ref, ch_sc = refs[n_pages:]
    rows = pages[0].shape[0]
    per_page = rows // CMP_STRIDE
    lane_blocks = 2 * KV_W // 128
    dst = lax.broadcasted_iota(jnp.int32, (rows, rows), 0)
    src = lax.broadcasted_iota(jnp.int32, (rows, rows), 1)
    perm = (src == (dst % per_page) * CMP_STRIDE + dst // per_page).astype(BF16)
    for p in range(n_pages):
        y = _dot(perm, pages[p][...].astype(BF16))
        for lb in range(lane_blocks):
            for rr in range(CMP_STRIDE):
                ch_sc[lb, p * per_page:(p + 1) * per_page, rr * 128:(rr + 1) * 128] = (
                    y[rr * per_page:(rr + 1) * per_page, lb * 128:(lb + 1) * 128])

    last = lax.broadcasted_iota(jnp.int32, (nc, 1), 0) == nc - 1
    for t, o_ref in enumerate((kc_ref, vc_ref)):
        posb = _dot(jnp.broadcast_to(pos_ref[t], (8, pos_ref.shape[2])).astype(BF16), w1f_ref[t])[0:1]
        for pair in range(N_KV // 2):
            ch = ch_sc[t * (N_KV // 2) + pair].astype(BF16)
            for j in range(2):
                hd = 2 * pair + j
                hh = _dot(ch, w1p_ref[t, j])
                lo = hh[:, :CMP_HID]
                hi = hh[:, CMP_HID:]
                hi = jnp.where(last, 0.0, jnp.concatenate([hi[1:], hi[:1]], axis=0))
                hid = jax.nn.gelu(lo + hi + posb)
                o_ref[:, hd * HEAD_DIM:(hd + 1) * HEAD_DIM] = _dot(hid.astype(BF16), w2_ref[t]).astype(BF16)


def _compress(src, page_table, pos, w1, w2):
    b, n_pages = page_table.shape
    rows = src.shape[1]
    nc = n_pages * rows // CMP_STRIDE
    half = CMP_STRIDE * HEAD_DIM
    w1c = jnp.concatenate([w1[:, :half], w1[:, half:]], axis=2).reshape(2, CMP_STRIDE, HEAD_DIM, 2 * CMP_HID)
    zeros = jnp.zeros_like(w1c)
    w1p = jnp.stack([jnp.concatenate([w1c, zeros], axis=2), jnp.concatenate([zeros, w1c], axis=2)], axis=1)
    w1p = w1p.reshape(2, 2, CMP_STRIDE * 128, 2 * CMP_HID).astype(BF16)
    consts = [pos.reshape(2, 1, 2 * half), w1p, w1.astype(BF16), w2.astype(BF16)]
    page_specs = [pl.BlockSpec((None, rows, 2 * KV_W), functools.partial(lambda p, i, pt: (pt[i, p], 0, 0), p))
                  for p in range(n_pages)]
    out = pl.BlockSpec((None, nc, KV_W), lambda i, pt: (i, 0, 0))
    grid_spec = pltpu.PrefetchScalarGridSpec(
        num_scalar_prefetch=1,
        grid=(b,),
        in_specs=page_specs + [pl.BlockSpec(c.shape, functools.partial(lambda n, i, pt: (0,) * n, c.ndim))
                               for c in consts],
        out_specs=[out, out],
        scratch_shapes=[pltpu.VMEM((2 * KV_W // 128, nc, CMP_STRIDE * 128), F32)],
    )
    return pl.pallas_call(
        functools.partial(_compress_body, n_pages, nc),
        grid_spec=grid_spec,
        out_shape=[jax.ShapeDtypeStruct((b, nc, KV_W), BF16)] * 2,
        compiler_params=_cparams("parallel"),
        name="nsa_compress",
    )(page_table, *([src] * n_pages), *consts)


def _query_rows(q, nq):
    pieces = []
    for kh in range(N_KV):
        for g in range(GROUP):
            hd = kh * GROUP + g
            x = q[:, hd * HEAD_DIM:(hd + 1) * HEAD_DIM] * (HEAD_DIM ** -0.5)
            parts = []
            if kh:
                parts.append(jnp.zeros((nq, kh * HEAD_DIM), F32))
            parts.append(x)
            if kh < N_KV - 1:
                parts.append(jnp.zeros((nq, (N_KV - 1 - kh) * HEAD_DIM), F32))
            pieces.append(jnp.concatenate(parts, axis=1))
    return jnp.concatenate(pieces, axis=0).astype(BF16)


def _row_meta(nq, t0):
    R = N_HEADS * nq
    rho = lax.broadcasted_iota(jnp.int32, (R, 1), 0)
    head = rho // nq
    slope = jnp.exp2(-8.0 * (head + 1).astype(F32) / N_HEADS)
    return slope, t0 + rho % nq


def _softmax_tiles(tiles):
    m = None
    for s, ok, _ in tiles:
        mt = jnp.max(jnp.where(ok, s, NEG_INF), axis=-1, keepdims=True)
        m = mt if m is None else jnp.maximum(m, mt)
    l = 0.0
    es = []
    for s, ok, _ in tiles:
        e = jnp.exp(jnp.where(ok, s, NEG_INF) - m)
        es.append(e)
        l = l + jnp.sum(e, axis=-1, keepdims=True)
    inv = 1.0 / l
    acc = 0.0
    ps = []
    for e, (s, ok, vv) in zip(es, tiles):
        p = jnp.where(ok, e * inv, 0.0)
        ps.append(p)
        if vv is not None:
            acc = acc + _dot(p.astype(BF16), vv)
    return acc, ps


def _select_blocks_shared(p_c, nq, nc, ns, ns_pad, cur, imp_sc):
    blk = lax.broadcasted_iota(jnp.int32, (ns_pad, nc), 0)
    cstart = lax.broadcasted_iota(jnp.int32, (ns_pad, nc), 1) * CMP_STRIDE
    overlap_t = ((cstart < (blk + 1) * SEL_BLOCK) & (cstart + CMP_LEN > blk * SEL_BLOCK)).astype(BF16)
    cols = []
    for kh in range(N_KV):
        acc = 0.0
        for g in range(GROUP):
            r0 = (kh * GROUP + g) * nq
            acc = acc + _dot_nt(overlap_t, p_c[r0:r0 + nq].astype(BF16))
        cols.append(acc)
    imp = jnp.concatenate(cols, axis=1)
    b = lax.broadcasted_iota(jnp.int32, (ns_pad, 1), 0)
    forced = (b == 0) | (b == cur) | (b == cur - 1)
    imp = jnp.where(b <= cur, jnp.where(forced, FORCED_SCORE, imp), REMOVED)
    imp_sc[...] = imp

    def count(s2, cnt):
        other = imp_sc[pl.ds(s2, 1), :]
        ahead = (other > imp) | ((other == imp) & (b > s2))
        return cnt + jnp.where(ahead, 1.0, 0.0)

    cnt = lax.fori_loop(0, cur + 1, count, jnp.zeros(imp.shape, F32))
    sel_t = jnp.where((cnt < min(N_SEL, ns)) & (b <= cur), 1.0, 0.0).astype(BF16)
    eye = (lax.broadcasted_iota(jnp.int32, (ns_pad, ns_pad), 0)
           == lax.broadcasted_iota(jnp.int32, (ns_pad, ns_pad), 1)).astype(BF16)
    rows = []
    for kh in range(N_KV):
        rows += [_dot_tn(sel_t[:, kh * nq:(kh + 1) * nq], eye)] * GROUP
    return jnp.concatenate(rows, axis=0)


def _expand_sel(sel_rows, ns_pad, key0, n_keys):
    blk = lax.broadcasted_iota(jnp.int32, (ns_pad, n_keys), 0)
    kb = (key0 + lax.broadcasted_iota(jnp.int32, (ns_pad, n_keys), 1)) // SEL_BLOCK
    return _dot(sel_rows.astype(BF16), (blk == kb).astype(BF16)) > 0.5


def _compressed_branch(qrows, kc, vc, slope, t_row, nc):
    pos_end = lax.broadcasted_iota(jnp.int32, (1, nc), 1) * CMP_STRIDE + (CMP_LEN - 1)
    d_c = t_row - pos_end
    ok_c = d_c >= 0
    s_c = _dot_nt(qrows, kc) - slope * d_c.astype(F32)
    acc, ps = _softmax_tiles([(s_c, ok_c, vc)])
    return acc, ps[0]


def _combine(gates, accs, nq):
    outs = []
    for kh in range(N_KV):
        for g in range(GROUP):
            hd = kh * GROUP + g
            r0 = hd * nq
            o = 0.0
            for c in range(3):
                gcol = gates[:, 3 * hd + c:3 * hd + c + 1]
                o = o + gcol * accs[c][r0:r0 + nq, kh * HEAD_DIM:(kh + 1) * HEAD_DIM]
            outs.append(o)
    return jnp.concatenate(outs, axis=1)


F_SEL, F_DIAG, F_OLD, F_POS = 0, SEL_BLOCK, 2 * SEL_BLOCK, 3 * SEL_BLOCK
MASK_BIG = 1e30


def _key_features(t):
    p = jnp.arange(t, dtype=jnp.int32)[:, None]
    c = jnp.arange(SEL_BLOCK, dtype=jnp.int32)[None, :]
    blk, off = p // SEL_BLOCK, p % SEL_BLOCK
    f_sel = jnp.where(blk == c, MASK_BIG, 0.0)
    f_diag = jnp.where(off > c, MASK_BIG, 0.0)
    f_old = jnp.where(off < c, MASK_BIG, 0.0)
    pair = jnp.where(c % 2 == 0, blk * SEL_BLOCK, off).astype(F32)
    f_pos = jnp.where(c < 6, pair, 0.0)
    return jnp.concatenate([f_sel, f_diag, f_old, f_pos], axis=1).astype(BF16)


def _attn_prompt_body(T, nc, ns, q_ref, gate_ref, kc_ref, vc_ref, ks_ref, vs_ref, kw_ref, vw_ref, kf_ref,
                      o_ref, m_sc, l_sc, acc_sc, s_sc, imp_sc):
    nq = q_ref.shape[0]
    R = N_HEADS * nq
    i = pl.program_id(1)
    t0 = pl.multiple_of(i * nq, nq)
    qrows = _query_rows(q_ref[...], nq)
    slope, t_row = _row_meta(nq, t0)

    acc_c, p_c = _compressed_branch(qrows, kc_ref[...], vc_ref[...], slope, t_row, nc)
    sel_rows = _select_blocks_shared(p_c, nq, nc, ns, SEL_BLOCK, i, imp_sc)

    lane = lax.broadcasted_iota(jnp.int32, (R, SEL_BLOCK), 1)
    q_loc = lax.broadcasted_iota(jnp.int32, (R, 1), 0) % nq
    s1 = slope.astype(BF16).astype(F32)
    s2 = (slope - s1).astype(BF16).astype(F32)
    s3 = slope - s1 - s2
    f_slope = jnp.where(lane < 2, s1, jnp.where(lane < 4, s2, jnp.where(lane < 6, s3, 0.0)))
    neg_hot = jnp.where(lane == q_loc, -1.0, 0.0)
    zero = jnp.zeros((R, SEL_BLOCK), F32)
    kt = min(KEY_TILE, T)
    d0 = pl.multiple_of(jnp.maximum(t0 - (kt - nq), 0), nq)
    upto_own = jnp.where(lane <= i, sel_rows, 0.0) - 1.0
    earlier = jnp.where(lane < d0 // SEL_BLOCK, sel_rows, 0.0) - 1.0

    def feat(a, b, c):
        return jnp.concatenate([a, b, c, f_slope], axis=1).astype(BF16)

    qf_main = feat(earlier, zero, zero)

    rd = lax.broadcasted_iota(jnp.int32, (kt, KV_W), 0)
    cd = lax.broadcasted_iota(jnp.int32, (kt, KV_W), 1)
    own = (rd >= t0 - d0) & (rd < t0 - d0 + nq)
    keep_d = (cd < F_DIAG) | (cd >= F_POS) | (own & (cd < F_OLD))
    kfd = jnp.where(keep_d, kf_ref[pl.ds(d0, kt), :].astype(F32), 0.0).astype(BF16)
    s_d = _dot_nt(qrows, ks_ref[pl.ds(d0, kt), :]) + _dot_nt(feat(upto_own, neg_hot, zero), kfd)
    m0 = jnp.max(s_d, axis=-1, keepdims=True)
    p0 = jnp.exp(s_d - m0)
    m_sc[...] = m0
    l_sc[...] = jnp.sum(p0, axis=-1, keepdims=True)
    acc_sc[...] = _dot(p0.astype(BF16), vs_ref[pl.ds(d0, kt), :])

    def scores(j):
        k0 = pl.multiple_of(jnp.minimum(j, T // kt - 1) * kt, kt)
        return _dot_nt(qrows, ks_ref[pl.ds(k0, kt), :]) + _dot_nt(qf_main, kf_ref[pl.ds(k0, kt), :])

    s_sc[0] = scores(0)

    def tile(j, carry):
        slot = j % 2
        s = s_sc[slot]
        s_sc[1 - slot] = scores(j + 1)
        k0 = pl.multiple_of(j * kt, kt)
        m_old = m_sc[...]
        m_new = jnp.maximum(m_old, jnp.max(s, axis=-1, keepdims=True))
        p = jnp.exp(s - m_new)
        scale = jnp.exp(m_old - m_new)
        l_sc[...] = scale * l_sc[...] + jnp.sum(p, axis=-1, keepdims=True)
        acc_sc[...] = scale * acc_sc[...] + _dot(p.astype(BF16), vs_ref[pl.ds(k0, kt), :])
        m_sc[...] = m_new
        return carry

    lax.fori_loop(0, (d0 + kt - 1) // kt, tile, 0)
    acc_s = acc_sc[...] * (1.0 / l_sc[...])

    nw = min(WINDOW + nq, T)

    def window_masked(_):
        w0 = pl.multiple_of(jnp.clip(t0 - WINDOW, 0, T - nw), nq)
        posw = w0 + lax.broadcasted_iota(jnp.int32, (1, nw), 1)
        d_w = t_row - posw
        ok_w = (d_w >= 0) & (d_w <= WINDOW)
        s_w = _dot_nt(qrows, kw_ref[pl.ds(w0, nw), :]) - slope * d_w.astype(F32)
        return _softmax_tiles([(s_w, ok_w, vw_ref[pl.ds(w0, nw), :])])[0]

    def window_full(_):
        w0 = pl.multiple_of(t0 - WINDOW, nq)
        rw = lax.broadcasted_iota(jnp.int32, (nw, KV_W), 0)
        cw = lax.broadcasted_iota(jnp.int32, (nw, KV_W), 1)
        keep = ((cw >= F_POS) | ((rw < nq) & (cw >= F_OLD) & (cw < F_POS))
                | ((rw >= WINDOW) & (cw >= F_DIAG) & (cw < F_OLD)))
        kfw = jnp.where(keep, kf_ref[pl.ds(w0, nw), :].astype(F32), 0.0).astype(BF16)
        s_w = _dot_nt(qrows, kw_ref[pl.ds(w0, nw), :]) + _dot_nt(feat(zero, neg_hot, neg_hot), kfw)
        e = jnp.exp(s_w - jnp.max(s_w, axis=-1, keepdims=True))
        acc = _dot(e.astype(BF16), vw_ref[pl.ds(w0, nw), :])
        return acc * (1.0 / jnp.sum(e, axis=-1, keepdims=True))

    if T >= 2 * WINDOW:
        acc_w = lax.cond(i >= WINDOW // nq, window_full, window_masked, 0)
    else:
        acc_w = window_masked(0)

    gates = jax.nn.sigmoid(gate_ref[...])
    o_ref[...] = _combine(gates, (acc_c, acc_s, acc_w), nq)


def _attn_prompt(z, kvb, kc, vc, nq):
    b, t, _ = z.shape
    nqd = N_HEADS * HEAD_DIM
    nc = kc.shape[1]
    ns = t // SEL_BLOCK
    R = N_HEADS * nq
    assert nq == SEL_BLOCK and ns <= SEL_BLOCK and KV_W == 4 * SEL_BLOCK
    kvspec = [pl.BlockSpec((None, t, KV_W), functools.partial(lambda c, i, j: (i, 0, c), c))
              for c in (2, 3, 4, 5)]
    return pl.pallas_call(
        functools.partial(_attn_prompt_body, t, nc, ns),
        grid=(b, t // nq),
        in_specs=[
            pl.BlockSpec((None, nq, nqd), lambda i, j: (i, j, 0)),
            pl.BlockSpec((None, nq, 128), lambda i, j: (i, j, nqd // 128)),
            pl.BlockSpec((None, nc, KV_W), lambda i, j: (i, 0, 0)),
            pl.BlockSpec((None, nc, KV_W), lambda i, j: (i, 0, 0)),
        ] + kvspec + [pl.BlockSpec((t, KV_W), lambda i, j: (0, 0))],
        out_specs=pl.BlockSpec((None, nq, nqd), lambda i, j: (i, j, 0)),
        out_shape=jax.ShapeDtypeStruct((b, t, nqd), F32),
        scratch_shapes=[pltpu.VMEM((R, 1), F32), pltpu.VMEM((R, 1), F32), pltpu.VMEM((R, KV_W), F32),
                        pltpu.VMEM((2, R, min(KEY_TILE, t)), F32), pltpu.VMEM((SEL_BLOCK, N_KV * nq), F32)],
        compiler_params=_cparams("parallel", "arbitrary"),
        name="nsa_attn_prompt",
    )(z, z, kc, vc, kvb, kvb, kvb, kvb, _key_features(t))


def _attn_sample_body(n_pages, page, past, tq, nc, ns, ns_pad, pt_ref, q_ref, gate_ref, kc_ref, vc_ref,
                      *rest):
    kpages = rest[:n_pages]
    vpages = rest[n_pages:2 * n_pages]
    kn_ref, vn_ref, kwn_ref, vwn_ref, kwp_ref, vwp_ref, o_ref, imp_sc = rest[2 * n_pages:]
    nq = q_ref.shape[0]
    nn = kn_ref.shape[0]
    qrows = _query_rows(q_ref[...], nq)
    slope, t_row = _row_meta(nq, past)
    t_row = jnp.minimum(t_row, past + tq - 1)

    acc_c, p_c = _compressed_branch(qrows, kc_ref[...], vc_ref[...], slope, t_row, nc)
    sel_rows = _select_blocks_shared(p_c, nq, nc, ns, ns_pad, past // SEL_BLOCK, imp_sc)

    def new_tile(k_ref, v_ref, sel):
        pos = past + lax.broadcasted_iota(jnp.int32, (1, nn), 1)
        d = t_row - pos
        ok = d >= 0
        if sel:
            ok = ok & _expand_sel(sel_rows, ns_pad, past, nn)
        s = _dot_nt(qrows, k_ref[...]) - slope * d.astype(F32)
        return s, ok, v_ref[...]

    tiles = []
    for p in range(n_pages):
        pos = p * page + lax.broadcasted_iota(jnp.int32, (1, page), 1)
        d = t_row - pos
        ok = (d >= 0) & _expand_sel(sel_rows, ns_pad, p * page, page)
        s = _dot_nt(qrows, kpages[p][...].astype(BF16)) - slope * d.astype(F32)
        tiles.append((s, ok, vpages[p][...].astype(BF16)))
    tiles.append(new_tile(kn_ref, vn_ref, True))
    acc_s, _ = _softmax_tiles(tiles)

    nwp = kwp_ref.shape[0]
    w_pos0 = past - nwp
    posw = w_pos0 + lax.broadcasted_iota(jnp.int32, (1, nwp), 1)
    d_w = t_row - posw
    ok_w = (d_w >= 0) & (d_w <= WINDOW)
    s_w = _dot_nt(qrows, kwp_ref[...].astype(BF16)) - slope * d_w.astype(F32)
    sn, okn, vn = new_tile(kwn_ref, vwn_ref, False)
    dn = t_row - (past + lax.broadcasted_iota(jnp.int32, (1, nn), 1))
    acc_w, _ = _softmax_tiles([(s_w, ok_w, vwp_ref[...].astype(BF16)), (sn, okn & (dn <= WINDOW), vn)])

    gates = jax.nn.sigmoid(gate_ref[...])
    o_ref[...] = _combine(gates, (acc_c, acc_s, acc_w), nq)


def _attn_sample(z, kvb, kc, vc, cache3, win3, page_table, tq):
    b, nq, _ = z.shape
    nqd = N_HEADS * HEAD_DIM
    n_pages = page_table.shape[1]
    page = cache3.shape[1]
    past = n_pages * page
    nn = kvb.shape[1]
    nc = kc.shape[1]
    ns = -(-(past + tq) // SEL_BLOCK)
    ns_pad = -(-ns // 128) * 128
    nwp = win3.shape[1]
    assert past % SEL_BLOCK == 0 and nn <= SEL_BLOCK and nwp == WINDOW
    assert (past + tq - 1) // SEL_BLOCK == past // SEL_BLOCK
    kp = [pl.BlockSpec((None, page, KV_W), functools.partial(lambda p, i, pt: (pt[i, p], 0, 2), p))
          for p in range(n_pages)]
    vp = [pl.BlockSpec((None, page, KV_W), functools.partial(lambda p, i, pt: (pt[i, p], 0, 3), p))
          for p in range(n_pages)]
    newspec = [pl.BlockSpec((None, nn, KV_W), functools.partial(lambda c, i, pt: (i, 0, c), c))
               for c in (2, 3, 4, 5)]
    grid_spec = pltpu.PrefetchScalarGridSpec(
        num_scalar_prefetch=1,
        grid=(b,),
        in_specs=[
            pl.BlockSpec((None, nq, nqd), lambda i, pt: (i, 0, 0)),
            pl.BlockSpec((None, nq, 128), lambda i, pt: (i, 0, nqd // 128)),
            pl.BlockSpec((None, nc, KV_W), lambda i, pt: (i, 0, 0)),
            pl.BlockSpec((None, nc, KV_W), lambda i, pt: (i, 0, 0)),
        ] + kp + vp + newspec + [
            pl.BlockSpec((None, nwp, KV_W), lambda i, pt: (i, 0, 0)),
            pl.BlockSpec((None, nwp, KV_W), lambda i, pt: (i, 0, 1)),
        ],
        out_specs=pl.BlockSpec((None, nq, nqd), lambda i, pt: (i, 0, 0)),
        scratch_shapes=[pltpu.VMEM((ns_pad, N_KV * nq), F32)],
    )
    return pl.pallas_call(
        functools.partial(_attn_sample_body, n_pages, page, past, tq, nc, ns, ns_pad),
        grid_spec=grid_spec,
        out_shape=jax.ShapeDtypeStruct((b, nq, nqd), F32),
        compiler_params=_cparams("parallel"),
        name="nsa_attn_sample",
    )(page_table, z, z, kc, vc, *([cache3] * (2 * n_pages)), kvb, kvb, kvb, kvb, win3, win3)


def kernel(x_prompt, x_sample, cache_cs, cache_win, state_shift, state_wkv, page_table, norm_g, ffn_w_in, ffn_w_out, rw_mu, rw_w_rkv, rw_w0, rw_w1, rw_w2, rw_a0, rw_a1, rw_a2, rw_g1, rw_g2, rw_k_k, rw_k_a, rw_r_k, rw_lnx_w, rw_lnx_b, rw_w_o, rw_v0, rw_v1, rw_v2, kv_norm_g, kv_w, cmp_pos, cmp_w1, cmp_w2, b_w_in, b_w_o, final_g):
    bp, tp, d = x_prompt.shape
    bs, ts, _ = x_sample.shape
    depth = norm_g.shape[0]
    n_a = rw_mu.shape[0]
    nh = d // RWKV_HEAD
    mp, ms = bp * tp, bs * ts
    n_pool, page = cache_cs.shape[:2]
    n_pages = page_table.shape[1]
    past = n_pages * page
    nqd = N_HEADS * HEAD_DIM
    ts_pad = 8
    assert tp % SCAN_CHUNK == 0 and tp % page == 0 and tp % SEL_BLOCK == 0
    nq_p = max(1, min(64, tp, 256 // bp))
    assert nq_p == SEL_BLOCK and ts <= ts_pad
    assert (past + ts - CMP_LEN) // CMP_STRIDE * CMP_STRIDE + CMP_LEN <= past

    cache3 = cache_cs.reshape(n_pool, page, 4 * KV_W)
    win3 = cache_win.reshape(bs, cache_win.shape[1], 2 * KV_W)

    def pad_tokens(t):
        return jnp.pad(t.reshape(bs, ts, -1), ((0, 0), (0, ts_pad - ts), (0, 0)))

    h = [x_prompt.reshape(mp, d), x_sample.reshape(ms, d)]
    shifts = [[], []]
    states = [[], []]
    v_first = [None, None]
    kv = kvb = kc = vc = [None, None]
    y = [None, None]
    for l in range(depth):
        g = norm_g[l]
        w_in0, w_out0 = ffn_w_in[l, 0].astype(BF16), ffn_w_out[l, 0].astype(BF16)
        w_in1, w_out1 = ffn_w_in[l, 1].astype(BF16), ffn_w_out[l, 1].astype(BF16)
        last = l == depth - 1
        if l < n_a:
            p = dict(mu=rw_mu[l], w_rkv=rw_w_rkv[l], w0=rw_w0[l], w1=rw_w1[l], w2=rw_w2[l], a0=rw_a0[l],
                     a1=rw_a1[l], a2=rw_a2[l], g1=rw_g1[l], g2=rw_g2[l], k_k=rw_k_k[l], k_a=rw_k_a[l])
            if l > 0:
                p.update(v0=rw_v0[l - 1], v1=rw_v1[l - 1], v2=rw_v2[l - 1])
            w_o = rw_w_o[l].astype(BF16)
        else:
            w_q = _pad_cols(b_w_in[l - n_a], nqd + 128).astype(BF16)
            w_o = b_w_o[l - n_a].astype(BF16)
        for s in range(2):
            h[s], xn = _ffn(h[s], g[0], w_in0, w_out0, g_post=g[1])
            if l < n_a:
                if s == 0:
                    shifts[s].append(xn.reshape(bp, tp, d)[:, -1])
                    pre = _rwkv_pre(xn, None, v_first[s], p, seq_len=tp)
                    s0, chunk = jnp.zeros((bp, nh, RWKV_HEAD, RWKV_HEAD), F32), SCAN_CHUNK
                    seqs = [t.reshape(bp, tp, d) for t in pre[:6]]
                else:
                    xs = xn.reshape(bs, ts, d)
                    shifts[s].append(xs[:, -1])
                    xprev = jnp.concatenate([state_shift[l][:, None, :], xs[:, :-1]], axis=1).reshape(ms, d)
                    pre = _rwkv_pre(xn, xprev, v_first[s], p)
                    s0, chunk = state_wkv[l], ts_pad
                    seqs = [pad_tokens(t) for t in pre[:6]]
                if l == 0:
                    v_first[s] = pre[3]
                ymix, st = _rwkv_scan(*seqs, rw_r_k[l], rw_lnx_w[l], rw_lnx_b[l], s0, chunk)
                states[s].append(st)
                ymix = ymix.reshape(mp, d) if s == 0 else ymix[:, :ts].reshape(ms, d)
                h[s] = _matmul_res(ymix, pre[6], w_o, h[s])
            else:
                z, _ = _norm_matmul(xn, None, w_q)
                if s == 0:
                    o = _attn_prompt(z.reshape(bp, tp, nqd + 128), kvb[s].reshape(bp, tp, 6 * KV_W),
                                     kc[s], vc[s], nq_p).reshape(mp, nqd)
                else:
                    o = _attn_sample(pad_tokens(z), pad_tokens(kvb[s]), kc[s], vc[s], cache3, win3,
                                     page_table, ts)[:, :ts].reshape(ms, nqd)
                h[s] = _matmul_res(o, None, w_o, h[s])
            res = _ffn(h[s], g[2], w_in1, w_out1, g_post=final_g if last else None)
            if last:
                h[s], y[s] = res
            else:
                h[s] = res
        if l == n_a - 1:
            kv_w_b = kv_w.astype(BF16)
            kv0, kvb0 = _norm_matmul(h[0], kv_norm_g, kv_w_b)
            kv1, kvb1 = _norm_matmul(h[1], kv_norm_g, kv_w_b)
            kv, kvb = [kv0, kv1], [kvb0, kvb1]
            ident = jnp.arange(mp // page, dtype=jnp.int32).reshape(bp, tp // page)
            kc0, vc0 = _compress(kv0.reshape(mp // page, page, 6 * KV_W), ident, cmp_pos, cmp_w1, cmp_w2)
            kc1, vc1 = _compress(cache3, page_table, cmp_pos, cmp_w1, cmp_w2)
            kc, vc = [kc0, kc1], [vc0, vc1]

    kv_p = kv[0].reshape(bp, tp, 6, N_KV, HEAD_DIM)
    kv_s = kv[1].reshape(bs, ts, 6, N_KV, HEAD_DIM)
    win_all = jnp.concatenate([cache_win, kv_s[:, :, 4:]], axis=1)
    s_win = win_all[:, win_all.shape[1] - cache_win.shape[1]:]
    return (y[0].reshape(bp, tp, d), y[1].reshape(bs, ts, d), jnp.stack(shifts[0]), jnp.stack(states[0]),
            kv_p[:, :, :4], kv_p[:, tp - min(WINDOW, tp):, 4:],
            jnp.stack(shifts[1]), jnp.stack(states[1]), kv_s[:, :, :4], s_win)
```

```python
import functools

import jax
import jax.numpy as jnp
from jax import lax
from jax.experimental import pallas as pl
from jax.experimental.pallas import tpu as pltpu

F32 = jnp.float32
BF16 = jnp.bfloat16

RWKV_HEAD = 64
GN_EPS = 64e-5
N_HEADS = 16
HEAD_DIM = 64
N_KV = 4
GROUP = N_HEADS // N_KV
KV_W = N_KV * HEAD_DIM
CMP_STRIDE = 16
CMP_LEN = 2 * CMP_STRIDE
CMP_HID = 2 * HEAD_DIM
SEL_BLOCK = 64
N_SEL = 16
WINDOW = 512
RMS_EPS = 1e-6
NEG_INF = -1e30
FORCED_SCORE = 1e4
REMOVED = -3e38

SCAN_CHUNK = 64
SCAN_SUB = 16
SCAN_HEADS = 16
P_GRAM, P_SOLVE, P_OUT, P_STATE = 1, 1, 1, 3
KEY_TILE = 512
VMEM_LIMIT = 56 * 1024 * 1024


def _cparams(*sem):
    return pltpu.CompilerParams(dimension_semantics=sem, vmem_limit_bytes=VMEM_LIMIT)


def _row_tile(m, cap):
    t = cap
    while t > 8 and m % t:
        t //= 2
    assert m % t == 0, (m, cap)
    return t


def _rms(x, g):
    return x * lax.rsqrt(jnp.mean(x * x, axis=-1, keepdims=True) + RMS_EPS) * g


def _dot(a, b):
    return jnp.dot(a, b, preferred_element_type=F32)


def _dot_nt(a, b):
    return lax.dot_general(a, b, (((1,), (1,)), ((), ())), preferred_element_type=F32)


def _dot_tn(a, b):
    return lax.dot_general(a, b, (((0,), (0,)), ((), ())), preferred_element_type=F32)


def _ffn_body(nf, post, h_ref, g_ref, wg_ref, wu_ref, wo_ref, gp_ref, *rest):
    if post:
        o_ref, on_ref, xn_sc, acc_sc = rest
    else:
        o_ref, xn_sc, acc_sc = rest
    f = pl.program_id(1)

    @pl.when(f == 0)
    def _():
        xn_sc[...] = _rms(h_ref[...], g_ref[...]).astype(BF16)
        acc_sc[...] = jnp.zeros_like(acc_sc)

    xn = xn_sc[...]
    gate = _dot(xn, wg_ref[...])
    up = _dot(xn, wu_ref[...])
    hid = (gate * jax.nn.sigmoid(gate) * up).astype(BF16)
    acc_sc[...] += _dot(hid, wo_ref[...])

    @pl.when(f == nf - 1)
    def _():
        hn = h_ref[...] + 0.5 * acc_sc[...]
        o_ref[...] = hn
        if post:
            on_ref[...] = _rms(hn, gp_ref[...])


def _ffn(h, g, w_in, w_out, g_post=None):
    m, d = h.shape
    dff = w_out.shape[0]
    tm = _row_tile(m, 512)
    tf = dff // 2 if (dff // 2) % 128 == 0 else dff
    nf = dff // tf
    post = g_post is not None
    gp = g_post if post else g
    out_shape = [jax.ShapeDtypeStruct((m, d), F32)]
    out_specs = [pl.BlockSpec((tm, d), lambda i, f: (i, 0))]
    if post:
        out_shape.append(jax.ShapeDtypeStruct((m, d), F32))
        out_specs.append(pl.BlockSpec((tm, d), lambda i, f: (i, 0)))
    res = pl.pallas_call(
        functools.partial(_ffn_body, nf, post),
        grid=(m // tm, nf),
        in_specs=[
            pl.BlockSpec((tm, d), lambda i, f: (i, 0)),
            pl.BlockSpec((1, d), lambda i, f: (0, 0)),
            pl.BlockSpec((d, tf), lambda i, f: (0, f)),
            pl.BlockSpec((d, tf), lambda i, f: (0, nf + f)),
            pl.BlockSpec((tf, d), lambda i, f: (f, 0)),
            pl.BlockSpec((1, d), lambda i, f: (0, 0)),
        ],
        out_specs=out_specs,
        out_shape=out_shape,
        scratch_shapes=[pltpu.VMEM((tm, d), BF16), pltpu.VMEM((tm, d), F32)],
        compiler_params=_cparams("parallel", "arbitrary"),
        name="ffn",
    )(h, g.reshape(1, d), w_in, w_in, w_out, gp.reshape(1, d))
    return res if post else res[0]


def _norm_matmul_body(norm, x_ref, g_ref, w_ref, o_ref, ob_ref):
    x = x_ref[...]
    if norm:
        x = _rms(x, g_ref[...])
    y = _dot(x.astype(BF16), w_ref[...])
    o_ref[...] = y
    ob_ref[...] = y.astype(BF16)


def _norm_matmul(x, g, w):
    m, d = x.shape
    n = w.shape[1]
    tm = _row_tile(m, 512)
    norm = g is not None
    gg = g.reshape(1, d) if norm else jnp.ones((1, d), F32)
    return pl.pallas_call(
        functools.partial(_norm_matmul_body, norm),
        grid=(m // tm,),
        in_specs=[
            pl.BlockSpec((tm, d), lambda i: (i, 0)),
            pl.BlockSpec((1, d), lambda i: (0, 0)),
            pl.BlockSpec((d, n), lambda i: (0, 0)),
        ],
        out_specs=[pl.BlockSpec((tm, n), lambda i: (i, 0)),
                   pl.BlockSpec((tm, n), lambda i: (i, 0))],
        out_shape=[jax.ShapeDtypeStruct((m, n), F32), jax.ShapeDtypeStruct((m, n), BF16)],
        compiler_params=_cparams("parallel"),
        name="norm_matmul",
    )(x, gg, w)


def _matmul_res_body(gated, x_ref, g_ref, w_ref, h_ref, o_ref):
    x = x_ref[...]
    if gated:
        x = x * g_ref[...]
    o_ref[...] = h_ref[...] + _dot(x.astype(BF16), w_ref[...])


def _matmul_res(x, gate, w, h):
    m, k = x.shape
    n = w.shape[1]
    tm = _row_tile(m, 512)
    gated = gate is not None
    gg = gate if gated else x
    return pl.pallas_call(
        functools.partial(_matmul_res_body, gated),
        grid=(m // tm,),
        in_specs=[
            pl.BlockSpec((tm, k), lambda i: (i, 0)),
            pl.BlockSpec((tm, k), lambda i: (i, 0)),
            pl.BlockSpec((k, n), lambda i: (0, 0)),
            pl.BlockSpec((tm, n), lambda i: (i, 0)),
        ],
        out_specs=pl.BlockSpec((tm, n), lambda i: (i, 0)),
        out_shape=jax.ShapeDtypeStruct((m, n), F32),
        compiler_params=_cparams("parallel"),
        name="matmul_res",
    )(x, gg, w, h)


def _softplus(x):
    return jnp.maximum(x, 0.0) + jnp.log1p(jnp.exp(-jnp.abs(x)))


def _rwkv_pre_body(vres, shift_rows, x_ref, xp_ref, vf_ref, mu_ref, wr_ref, wk_ref, wv_ref,
                   w0_ref, w1_ref, w2_ref, a0_ref, a1_ref, a2_ref, g1_ref, g2_ref,
                   v0_ref, v1_ref, v2_ref, kk_ref, ka_ref,
                   r_o, lw_o, k_o, v_o, kkr_o, a_o, g_o):
    x = x_ref[...]
    if shift_rows:
        first = lax.broadcasted_iota(jnp.int32, x.shape, 0) == 0
        xp = jnp.where(first, xp_ref[...], pltpu.roll(x, 1, axis=0))
    else:
        xp = xp_ref[...]
    xx = xp - x

    def lerp(c):
        return (x + xx * mu_ref[c:c + 1, :]).astype(BF16)

    r = _dot(lerp(0), wr_ref[...])
    k = _dot(lerp(1), wk_ref[...])
    xv = lerp(2)
    v = _dot(xv, wv_ref[...])
    w = w0_ref[...] + _dot(jnp.tanh(_dot(lerp(3), w1_ref[...])).astype(BF16), w2_ref[...])
    w = -_softplus(-w) - 0.5
    a = jax.nn.sigmoid(a0_ref[...] + _dot(_dot(lerp(4), a1_ref[...]).astype(BF16), a2_ref[...]))
    g = _dot(jax.nn.sigmoid(_dot(lerp(5), g1_ref[...])).astype(BF16), g2_ref[...])
    if vres:
        mix = jax.nn.sigmoid(v0_ref[...] + _dot(_dot(xv, v1_ref[...]).astype(BF16), v2_ref[...]))
        v = v + (vf_ref[...] - v) * mix
    r_o[...] = r
    lw_o[...] = -jnp.exp(w)
    k_o[...] = k * (1.0 + (a - 1.0) * ka_ref[...])
    v_o[...] = v
    kkr_o[...] = k * kk_ref[...]
    a_o[...] = a
    g_o[...] = g


def _pad_cols(w, n):
    return jnp.pad(w, ((0, 0), (0, n - w.shape[1])))


def _pad_rows(w, n):
    return jnp.pad(w, ((0, n - w.shape[0]), (0, 0)))


def _lora(w_down, w_up):
    r = w_down.shape[1]
    rp = -(-r // 128) * 128
    return _pad_cols(w_down, rp).astype(BF16), _pad_rows(w_up, rp).astype(BF16)


def _rwkv_pre(xn, xprev, v_first, p, seq_len=None):
    m, d = xn.shape
    tm = _row_tile(m, 256)
    shift_rows = xprev is None
    if shift_rows:
        assert seq_len % tm == 0
        n_tiles = m // tm
        tails = xn.reshape(n_tiles, tm, d)[:, tm - 1]
        prev = jnp.concatenate([jnp.zeros((1, d), F32), tails[:-1]], axis=0)
        starts = (jnp.arange(n_tiles) * tm) % seq_len == 0
        xprev = jnp.where(starts[:, None], 0.0, prev).reshape(n_tiles, 1, d)
    vres = v_first is not None
    vf = v_first if vres else xn
    w1, w2 = _lora(p["w1"], p["w2"])
    a1, a2 = _lora(p["a1"], p["a2"])
    g1, g2 = _lora(p["g1"], p["g2"])
    if vres:
        v1, v2 = _lora(p["v1"], p["v2"])
        v0 = p["v0"].reshape(1, d)
    else:
        v1, v2, v0 = a1, a2, p["a0"].reshape(1, d)
    tok = pl.BlockSpec((tm, d), lambda i: (i, 0))

    def full(a):
        return pl.BlockSpec(a.shape, lambda i: (0,) * a.ndim)

    wr, wk, wv = (p["w_rkv"][c].astype(BF16) for c in range(3))
    consts = [p["mu"], wr, wk, wv, p["w0"].reshape(1, d), w1, w2, p["a0"].reshape(1, d), a1, a2,
              g1, g2, v0, v1, v2, p["k_k"].reshape(1, d), p["k_a"].reshape(1, d)]
    return pl.pallas_call(
        functools.partial(_rwkv_pre_body, vres, shift_rows),
        grid=(m // tm,),
        in_specs=[tok, pl.BlockSpec((None, 1, d), lambda i: (i, 0, 0)) if shift_rows else tok, tok]
        + [full(c) for c in consts],
        out_specs=[tok] * 7,
        out_shape=[jax.ShapeDtypeStruct((m, d), F32)] * 7,
        compiler_params=_cparams("parallel"),
        name="rwkv_pre",
    )(xn, xprev, vf, *consts)


def _split(x):
    hi = x.astype(BF16)
    lo = (x - hi.astype(F32)).astype(BF16)
    return hi, lo


_BATCH_DIMS = {"nn": ((2,), (1,)), "nt": ((2,), (2,)), "tn": ((1,), (1,))}


def _bmm(kind, a, b, passes):
    dn = (_BATCH_DIMS[kind], ((0,), (0,)))

    def dg(x, y):
        return lax.dot_general(x, y, dn, preferred_element_type=F32)

    if passes == 1:
        return dg(a.astype(BF16), b.astype(BF16))
    ah, al = _split(a)
    bh, bl = _split(b)
    return dg(ah, bh) + (dg(ah, bl) + dg(al, bh))


def _mm_exact_lhs(a01, b):
    b1 = b.astype(BF16)
    r1 = b - b1.astype(F32)
    b2 = r1.astype(BF16)
    b3 = (r1 - b2.astype(F32)).astype(BF16)
    return _dot(a01, b1) + (_dot(a01, b2) + _dot(a01, b3))


def _scan_body(L, n_heads, r_ref, lw_ref, k_ref, v_ref, kkr_ref, a_ref, rk_ref, lnw_ref, lnb_ref,
               s0_ref, y_ref, st_ref, s_sc):
    ci = pl.program_id(1)
    N = RWKV_HEAD
    HB = min(SCAN_HEADS, n_heads)

    @pl.when(ci == 0)
    def _():
        s_sc[...] = s0_ref[...]

    row = lax.broadcasted_iota(jnp.int32, (1, L, L), 1)
    col = lax.broadcasted_iota(jnp.int32, (1, L, L), 2)
    tri_incl = row >= col
    tri_strict = row > col
    same_sub = (row // SCAN_SUB) == (col // SCAN_SUB)
    ltri = tri_incl[0].astype(BF16)
    eye = (row == col).astype(F32)
    eye_n = (lax.broadcasted_iota(jnp.int32, (1, N, N), 1)
             == lax.broadcasted_iota(jnp.int32, (1, N, N), 2)).astype(F32)

    lw = lw_ref[...]
    c = _mm_exact_lhs(ltri, lw)
    ec = jnp.exp(c)
    eci = jnp.exp(-c)
    ecp = jnp.exp(c - lw)
    r = r_ref[...]
    k = k_ref[...]
    kt_all = k * eci
    rt_all = r * ec
    be_all = a_ref[...] * eci
    rkr_all = r * k * rk_ref[...]

    for g in range(n_heads // HB):
        def heads(x):
            return jnp.stack([x[:, (g * HB + i) * N:(g * HB + i + 1) * N] for i in range(HB)])

        kkr = heads(kkr_ref[...])
        kk = kkr * lax.rsqrt(jnp.maximum(jnp.sum(kkr * kkr, axis=-1, keepdims=True), 1e-24))
        at = -kk * heads(ecp)
        bt = kk * heads(be_all)
        kt = heads(kt_all)
        rt = heads(rt_all)
        v = heads(v_ref[...])
        ab = jnp.concatenate([at, rt], axis=1)
        gram_b = _bmm("nt", ab, bt, P_GRAM)
        gram_k = _bmm("nt", ab, kt, P_GRAM)
        nab = jnp.where(tri_strict, gram_b[:, :L], 0.0)
        aak = jnp.where(tri_strict, gram_k[:, :L], 0.0)
        gb = jnp.where(tri_incl, gram_b[:, L:], 0.0)
        gk = jnp.where(tri_incl, gram_k[:, L:], 0.0)
        nd = jnp.where(same_sub, nab, 0.0)
        noff = nab - nd
        nd2 = _bmm("nn", nd, nd, P_SOLVE)
        nd4 = _bmm("nn", nd2, nd2, P_SOLVE)
        nd8 = _bmm("nn", nd4, nd4, P_SOLVE)
        td = eye + nd
        td = td + _bmm("nn", td, nd2, P_SOLVE)
        td = td + _bmm("nn", td, nd4, P_SOLVE)
        td = td + _bmm("nn", td, nd8, P_SOLVE)
        rhs = jnp.concatenate([at, _bmm("nn", aak, v, P_SOLVE)], axis=2)
        wz = _bmm("nn", td, rhs, P_SOLVE)
        if L > SCAN_SUB:
            assert L // SCAN_SUB <= 4
            mo = _bmm("nn", td, noff, P_SOLVE)
            wz = wz + _bmm("nn", mo, wz, P_SOLVE)
            wz = wz + _bmm("nn", _bmm("nn", mo, mo, P_SOLVE), wz, P_SOLVE)
        gwz = _bmm("nn", gb, wz, P_OUT)
        rp = rt + gwz[:, :, :N]
        y0 = gwz[:, :, N:] + _bmm("nn", gk, v, P_OUT)
        ecl = heads(ec[L - 1:L, :])
        wzb = _bmm("tn", wz, bt, P_STATE)
        pm = (eye_n + wzb[:, :N]) * ecl
        qm = (wzb[:, N:] + _bmm("tn", v, kt, P_STATE)) * ecl
        s_prev = s_sc[g * HB:(g + 1) * HB]
        y = _bmm("nt", rp, s_prev, P_OUT) + y0
        s_sc[g * HB:(g + 1) * HB] = _bmm("nn", s_prev, pm, P_STATE) + qm
        mean = jnp.mean(y, axis=-1, keepdims=True)
        yc = y - mean
        var = jnp.mean(yc * yc, axis=-1, keepdims=True)
        yn = yc * lax.rsqrt(var + GN_EPS) * heads(lnw_ref[...]) + heads(lnb_ref[...])
        out = yn + jnp.sum(heads(rkr_all), axis=-1, keepdims=True) * v
        y_ref[:, g * HB * N:(g + 1) * HB * N] = jnp.concatenate([out[i] for i in range(HB)], axis=1)

    @pl.when(ci == pl.num_programs(1) - 1)
    def _():
        st_ref[...] = s_sc[...]


def _rwkv_scan(r, lw, k, v, kkr, a, r_k, lnx_w, lnx_b, s0, L):
    b, t, d = r.shape
    nh = d // RWKV_HEAD
    tok = pl.BlockSpec((None, L, d), lambda i, c: (i, c, 0))
    vec = pl.BlockSpec((1, d), lambda i, c: (0, 0))
    st = pl.BlockSpec((None, nh, RWKV_HEAD, RWKV_HEAD), lambda i, c: (i, 0, 0, 0))
    return pl.pallas_call(
        functools.partial(_scan_body, L, nh),
        grid=(b, t // L),
        in_specs=[tok] * 6 + [vec] * 3 + [st],
        out_specs=[tok, st],
        out_shape=[jax.ShapeDtypeStruct((b, t, d), F32),
                   jax.ShapeDtypeStruct((b, nh, RWKV_HEAD, RWKV_HEAD), F32)],
        scratch_shapes=[pltpu.VMEM((nh, RWKV_HEAD, RWKV_HEAD), F32)],
        compiler_params=_cparams("parallel", "arbitrary"),
        name="rwkv_scan",
    )(r, lw, k, v, kkr, a, r_k.reshape(1, d), lnx_w.reshape(1, d), lnx_b.reshape(1, d), s0)


def _compress_body(n_pages, nc, pt_ref, *refs):
    pages = refs[:n_pages]
    pos_ref, w1p_ref, w1f_ref, w2_ref, kc_ref, vc_ref, ch_sc = refs[n_pages:]
    rows = pages[0].shape[0]
    per_page = rows // CMP_STRIDE
    lane_blocks = 2 * KV_W // 128
    dst = lax.broadcasted_iota(jnp.int32, (rows, rows), 0)
    src = lax.broadcasted_iota(jnp.int32, (rows, rows), 1)
    perm = (src == (dst % per_page) * CMP_STRIDE + dst // per_page).astype(BF16)
    for p in range(n_pages):
        y = _dot(perm, pages[p][...].astype(BF16))
        for lb in range(lane_blocks):
            for rr in range(CMP_STRIDE):
                ch_sc[lb, p * per_page:(p + 1) * per_page, rr * 128:(rr + 1) * 128] = (
                    y[rr * per_page:(rr + 1) * per_page, lb * 128:(lb + 1) * 128])

    last = lax.broadcasted_iota(jnp.int32, (nc, 1), 0) == nc - 1
    for t, o_ref in enumerate((kc_ref, vc_ref)):
        posb = _dot(jnp.broadcast_to(pos_ref[t], (8, pos_ref.shape[2])).astype(BF16), w1f_ref[t])[0:1]
        for pair in range(N_KV // 2):
            ch = ch_sc[t * (N_KV // 2) + pair].astype(BF16)
            for j in range(2):
                hd = 2 * pair + j
                hh = _dot(ch, w1p_ref[t, j])
                lo = hh[:, :CMP_HID]
                hi = hh[:, CMP_HID:]
                hi = jnp.where(last, 0.0, jnp.concatenate([hi[1:], hi[:1]], axis=0))
                hid = jax.nn.gelu(lo + hi + posb)
                o_ref[:, hd * HEAD_DIM:(hd + 1) * HEAD_DIM] = _dot(hid.astype(BF16), w2_ref[t]).astype(BF16)


def _compress(src, page_table, pos, w1, w2):
    b, n_pages = page_table.shape
    rows = src.shape[1]
    nc = n_pages * rows // CMP_STRIDE
    half = CMP_STRIDE * HEAD_DIM
    w1c = jnp.concatenate([w1[:, :half], w1[:, half:]], axis=2).reshape(2, CMP_STRIDE, HEAD_DIM, 2 * CMP_HID)
    zeros = jnp.zeros_like(w1c)
    w1p = jnp.stack([jnp.concatenate([w1c, zeros], axis=2), jnp.concatenate([zeros, w1c], axis=2)], axis=1)
    w1p = w1p.reshape(2, 2, CMP_STRIDE * 128, 2 * CMP_HID).astype(BF16)
    consts = [pos.reshape(2, 1, 2 * half), w1p, w1.astype(BF16), w2.astype(BF16)]
    page_specs = [pl.BlockSpec((None, rows, 2 * KV_W), functools.partial(lambda p, i, pt: (pt[i, p], 0, 0), p))
                  for p in range(n_pages)]
    out = pl.BlockSpec((None, nc, KV_W), lambda i, pt: (i, 0, 0))
    grid_spec = pltpu.PrefetchScalarGridSpec(
        num_scalar_prefetch=1,
        grid=(b,),
        in_specs=page_specs + [pl.BlockSpec(c.shape, functools.partial(lambda n, i, pt: (0,) * n, c.ndim))
                               for c in consts],
        out_specs=[out, out],
        scratch_shapes=[pltpu.VMEM((2 * KV_W // 128, nc, CMP_STRIDE * 128), F32)],
    )
    return pl.pallas_call(
        functools.partial(_compress_body, n_pages, nc),
        grid_spec=grid_spec,
        out_shape=[jax.ShapeDtypeStruct((b, nc, KV_W), BF16)] * 2,
        compiler_params=_cparams("parallel"),
        name="nsa_compress",
    )(page_table, *([src] * n_pages), *consts)


def _query_rows(q, nq):
    pieces = []
    for kh in range(N_KV):
        for g in range(GROUP):
            hd = kh * GROUP + g
            x = q[:, hd * HEAD_DIM:(hd + 1) * HEAD_DIM] * (HEAD_DIM ** -0.5)
            parts = []
            if kh:
                parts.append(jnp.zeros((nq, kh * HEAD_DIM), F32))
            parts.append(x)
            if kh < N_KV - 1:
                parts.append(jnp.zeros((nq, (N_KV - 1 - kh) * HEAD_DIM), F32))
            pieces.append(jnp.concatenate(parts, axis=1))
    return jnp.concatenate(pieces, axis=0).astype(BF16)


def _row_meta(nq, t0):
    R = N_HEADS * nq
    rho = lax.broadcasted_iota(jnp.int32, (R, 1), 0)
    head = rho // nq
    slope = jnp.exp2(-8.0 * (head + 1).astype(F32) / N_HEADS)
    return slope, t0 + rho % nq


def _softmax_tiles(tiles):
    m = None
    for s, ok, _ in tiles:
        mt = jnp.max(jnp.where(ok, s, NEG_INF), axis=-1, keepdims=True)
        m = mt if m is None else jnp.maximum(m, mt)
    l = 0.0
    es = []
    for s, ok, _ in tiles:
        e = jnp.exp(jnp.where(ok, s, NEG_INF) - m)
        es.append(e)
        l = l + jnp.sum(e, axis=-1, keepdims=True)
    inv = 1.0 / l
    acc = 0.0
    ps = []
    for e, (s, ok, vv) in zip(es, tiles):
        p = jnp.where(ok, e * inv, 0.0)
        ps.append(p)
        if vv is not None:
            acc = acc + _dot(p.astype(BF16), vv)
    return acc, ps


def _select_blocks_shared(p_c, nq, nc, ns, ns_pad, cur, imp_sc):
    blk = lax.broadcasted_iota(jnp.int32, (ns_pad, nc), 0)
    cstart = lax.broadcasted_iota(jnp.int32, (ns_pad, nc), 1) * CMP_STRIDE
    overlap_t = ((cstart < (blk + 1) * SEL_BLOCK) & (cstart + CMP_LEN > blk * SEL_BLOCK)).astype(BF16)
    cols = []
    for kh in range(N_KV):
        acc = 0.0
        for g in range(GROUP):
            r0 = (kh * GROUP + g) * nq
            acc = acc + _dot_nt(overlap_t, p_c[r0:r0 + nq].astype(BF16))
        cols.append(acc)
    imp = jnp.concatenate(cols, axis=1)
    b = lax.broadcasted_iota(jnp.int32, (ns_pad, 1), 0)
    forced = (b == 0) | (b == cur) | (b == cur - 1)
    imp = jnp.where(b <= cur, jnp.where(forced, FORCED_SCORE, imp), REMOVED)
    imp_sc[...] = imp

    def count(s2, cnt):
        other = imp_sc[pl.ds(s2, 1), :]
        ahead = (other > imp) | ((other == imp) & (b > s2))
        return cnt + jnp.where(ahead, 1.0, 0.0)

    cnt = lax.fori_loop(0, cur + 1, count, jnp.zeros(imp.shape, F32))
    sel_t = jnp.where((cnt < min(N_SEL, ns)) & (b <= cur), 1.0, 0.0).astype(BF16)
    eye = (lax.broadcasted_iota(jnp.int32, (ns_pad, ns_pad), 0)
           == lax.broadcasted_iota(jnp.int32, (ns_pad, ns_pad), 1)).astype(BF16)
    rows = []
    for kh in range(N_KV):
        rows += [_dot_tn(sel_t[:, kh * nq:(kh + 1) * nq], eye)] * GROUP
    return jnp.concatenate(rows, axis=0)


def _expand_sel(sel_rows, ns_pad, key0, n_keys):
    blk = lax.broadcasted_iota(jnp.int32, (ns_pad, n_keys), 0)
    kb = (key0 + lax.broadcasted_iota(jnp.int32, (ns_pad, n_keys), 1)) // SEL_BLOCK
    return _dot(sel_rows.astype(BF16), (blk == kb).astype(BF16)) > 0.5


def _compressed_branch(qrows, kc, vc, slope, t_row, nc):
    pos_end = lax.broadcasted_iota(jnp.int32, (1, nc), 1) * CMP_STRIDE + (CMP_LEN - 1)
    d_c = t_row - pos_end
    ok_c = d_c >= 0
    s_c = _dot_nt(qrows, kc) - slope * d_c.astype(F32)
    acc, ps = _softmax_tiles([(s_c, ok_c, vc)])
    return acc, ps[0]


def _combine(gates, accs, nq):
    outs = []
    for kh in range(N_KV):
        for g in range(GROUP):
            hd = kh * GROUP + g
            r0 = hd * nq
            o = 0.0
            for c in range(3):
                gcol = gates[:, 3 * hd + c:3 * hd + c + 1]
                o = o + gcol * accs[c][r0:r0 + nq, kh * HEAD_DIM:(kh + 1) * HEAD_DIM]
            outs.append(o)
    return jnp.concatenate(outs, axis=1)


F_SEL, F_DIAG, F_OLD, F_POS = 0, SEL_BLOCK, 2 * SEL_BLOCK, 3 * SEL_BLOCK
MASK_BIG = 1e30


def _key_features(t):
    p = jnp.arange(t, dtype=jnp.int32)[:, None]
    c = jnp.arange(SEL_BLOCK, dtype=jnp.int32)[None, :]
    blk, off = p // SEL_BLOCK, p % SEL_BLOCK
    f_sel = jnp.where(blk == c, MASK_BIG, 0.0)
    f_diag = jnp.where(off > c, MASK_BIG, 0.0)
    f_old = jnp.where(off < c, MASK_BIG, 0.0)
    pair = jnp.where(c % 2 == 0, blk * SEL_BLOCK, off).astype(F32)
    f_pos = jnp.where(c < 6, pair, 0.0)
    return jnp.concatenate([f_sel, f_diag, f_old, f_pos], axis=1).astype(BF16)


def _attn_prompt_body(T, nc, ns, q_ref, gate_ref, kc_ref, vc_ref, ksa_ref, vs_ref, kw_ref, vw_ref, kf_ref,
                      o_ref, m_sc, l_sc, acc_sc, s_sc, imp_sc):
    nq = q_ref.shape[0]
    R = N_HEADS * nq
    i = pl.program_id(1)
    t0 = pl.multiple_of(i * nq, nq)
    qrows = _query_rows(q_ref[...], nq)
    slope, t_row = _row_meta(nq, t0)

    acc_c, p_c = _compressed_branch(qrows, kc_ref[...], vc_ref[...], slope, t_row, nc)
    sel_rows = _select_blocks_shared(p_c, nq, nc, ns, SEL_BLOCK, i, imp_sc)

    lane = lax.broadcasted_iota(jnp.int32, (R, SEL_BLOCK), 1)
    q_loc = lax.broadcasted_iota(jnp.int32, (R, 1), 0) % nq
    s1 = slope.astype(BF16).astype(F32)
    s2 = (slope - s1).astype(BF16).astype(F32)
    s3 = slope - s1 - s2
    f_slope = jnp.where(lane < 2, s1, jnp.where(lane < 4, s2, jnp.where(lane < 6, s3, 0.0)))
    neg_hot = jnp.where(lane == q_loc, -1.0, 0.0)
    zero = jnp.zeros((R, SEL_BLOCK), F32)
    kt = min(KEY_TILE, T)
    d0 = pl.multiple_of(jnp.maximum(t0 - (kt - nq), 0), nq)
    upto_own = jnp.where(lane <= i, sel_rows, 0.0) - 1.0
    earlier = jnp.where(lane < d0 // SEL_BLOCK, sel_rows, 0.0) - 1.0

    def feat(a, b, c):
        return jnp.concatenate([a, b, c, f_slope], axis=1).astype(BF16)

    q_in = q_ref[...]
    q_own = jnp.concatenate([q_in[:, hd * HEAD_DIM:(hd + 1) * HEAD_DIM] for hd in range(N_HEADS)],
                            axis=0) * (HEAD_DIM ** -0.5)

    def slc_rows(sel_m1):
        return jnp.concatenate([q_own, sel_m1, f_slope, zero], axis=1).astype(BF16)

    rows_kv = GROUP * nq

    def slc_scores(qa, k0):
        return jnp.concatenate(
            [_dot_nt(qa[kh * rows_kv:(kh + 1) * rows_kv], ksa_ref[kh, pl.ds(k0, kt), :]) for kh in range(N_KV)],
            axis=0)

    qa_main = slc_rows(earlier)

    pos_d = d0 + lax.broadcasted_iota(jnp.int32, (1, kt), 1)
    s_d = jnp.where(pos_d > t_row, NEG_INF, slc_scores(slc_rows(upto_own), d0))
    m0 = jnp.max(s_d, axis=-1, keepdims=True)
    p0 = jnp.exp(s_d - m0)
    m_sc[...] = m0
    l_sc[...] = jnp.sum(p0, axis=-1, keepdims=True)
    acc_sc[...] = _dot(p0.astype(BF16), vs_ref[pl.ds(d0, kt), :])

    def scores(j):
        return slc_scores(qa_main, pl.multiple_of(jnp.minimum(j, T // kt - 1) * kt, kt))

    s_sc[0] = scores(0)

    def tile(j, carry):
        slot = j % 2
        s = s_sc[slot]
        s_sc[1 - slot] = scores(j + 1)
        k0 = pl.multiple_of(j * kt, kt)
        m_old = m_sc[...]
        m_new = jnp.maximum(m_old, jnp.max(s, axis=-1, keepdims=True))
        p = jnp.exp(s - m_new)
        scale = jnp.exp(m_old - m_new)
        l_sc[...] = scale * l_sc[...] + jnp.sum(p, axis=-1, keepdims=True)
        acc_sc[...] = scale * acc_sc[...] + _dot(p.astype(BF16), vs_ref[pl.ds(k0, kt), :])
        m_sc[...] = m_new
        return carry

    lax.fori_loop(0, (d0 + kt - 1) // kt, tile, 0)
    acc_s = acc_sc[...] * (1.0 / l_sc[...])

    nw = min(WINDOW + nq, T)

    def window_masked(_):
        w0 = pl.multiple_of(jnp.clip(t0 - WINDOW, 0, T - nw), nq)
        posw = w0 + lax.broadcasted_iota(jnp.int32, (1, nw), 1)
        d_w = t_row - posw
        ok_w = (d_w >= 0) & (d_w <= WINDOW)
        s_w = _dot_nt(qrows, kw_ref[pl.ds(w0, nw), :]) - slope * d_w.astype(F32)
        return _softmax_tiles([(s_w, ok_w, vw_ref[pl.ds(w0, nw), :])])[0]

    def window_full(_):
        w0 = pl.multiple_of(t0 - WINDOW, nq)
        rw = lax.broadcasted_iota(jnp.int32, (nw, KV_W), 0)
        cw = lax.broadcasted_iota(jnp.int32, (nw, KV_W), 1)
        keep = ((cw >= F_POS) | ((rw < nq) & (cw >= F_OLD) & (cw < F_POS))
                | ((rw >= WINDOW) & (cw >= F_DIAG) & (cw < F_OLD)))
        kfw = jnp.where(keep, kf_ref[pl.ds(w0, nw), :].astype(F32), 0.0).astype(BF16)
        s_w = _dot_nt(qrows, kw_ref[pl.ds(w0, nw), :]) + _dot_nt(feat(zero, neg_hot, neg_hot), kfw)
        e = jnp.exp(s_w - jnp.max(s_w, axis=-1, keepdims=True))
        acc = _dot(e.astype(BF16), vw_ref[pl.ds(w0, nw), :])
        return acc * (1.0 / jnp.sum(e, axis=-1, keepdims=True))

    if T >= 2 * WINDOW:
        acc_w = lax.cond(i >= WINDOW // nq, window_full, window_masked, 0)
    else:
        acc_w = window_masked(0)

    gates = jax.nn.sigmoid(gate_ref[...])
    o_ref[...] = _combine(gates, (acc_c, acc_s, acc_w), nq)


def _attn_prompt(z, kvb, kc, vc, nq):
    b, t, _ = z.shape
    nqd = N_HEADS * HEAD_DIM
    nc = kc.shape[1]
    ns = t // SEL_BLOCK
    R = N_HEADS * nq
    assert nq == SEL_BLOCK and ns <= SEL_BLOCK and KV_W == 4 * SEL_BLOCK
    once = pl.Buffered(1)
    kvspec = [pl.BlockSpec((None, t, KV_W), functools.partial(lambda c, i, j: (i, 0, c), c), pipeline_mode=once)
              for c in (3, 4, 5)]
    kf = _key_features(t)
    k_slc = kvb[:, :, 2 * KV_W:3 * KV_W].reshape(b, t, N_KV, HEAD_DIM).transpose(0, 2, 1, 3)
    extra = jnp.concatenate([kf[:, F_SEL:F_DIAG], kf[:, F_POS:], jnp.zeros((t, SEL_BLOCK), BF16)], axis=1)
    ksa = jnp.concatenate([k_slc, jnp.broadcast_to(extra, (b, N_KV, t, 3 * SEL_BLOCK))], axis=3)
    return pl.pallas_call(
        functools.partial(_attn_prompt_body, t, nc, ns),
        grid=(b, t // nq),
        in_specs=[
            pl.BlockSpec((None, nq, nqd), lambda i, j: (i, j, 0)),
            pl.BlockSpec((None, nq, 128), lambda i, j: (i, j, nqd // 128)),
            pl.BlockSpec((None, nc, KV_W), lambda i, j: (i, 0, 0)),
            pl.BlockSpec((None, nc, KV_W), lambda i, j: (i, 0, 0)),
            pl.BlockSpec((None, N_KV, t, KV_W), lambda i, j: (i, 0, 0, 0), pipeline_mode=once),
        ] + kvspec + [pl.BlockSpec((t, KV_W), lambda i, j: (0, 0), pipeline_mode=once)],
        out_specs=pl.BlockSpec((None, nq, nqd), lambda i, j: (i, j, 0)),
        out_shape=jax.ShapeDtypeStruct((b, t, nqd), F32),
        scratch_shapes=[pltpu.VMEM((R, 1), F32), pltpu.VMEM((R, 1), F32), pltpu.VMEM((R, KV_W), F32),
                        pltpu.VMEM((2, R, min(KEY_TILE, t)), F32), pltpu.VMEM((SEL_BLOCK, N_KV * nq), F32)],
        compiler_params=_cparams("parallel", "arbitrary"),
        name="nsa_attn_prompt",
    )(z, z, kc, vc, ksa, kvb, kvb, kvb, kf)


def _attn_sample_body(n_pages, page, past, tq, nc, ns, ns_pad, pt_ref, q_ref, gate_ref, kc_ref, vc_ref,
                      *rest):
    kpages = rest[:n_pages]
    vpages = rest[n_pages:2 * n_pages]
    kn_ref, vn_ref, kwn_ref, vwn_ref, kwp_ref, vwp_ref, o_ref, imp_sc = rest[2 * n_pages:]
    nq = q_ref.shape[0]
    nn = kn_ref.shape[0]
    qrows = _query_rows(q_ref[...], nq)
    slope, t_row = _row_meta(nq, past)
    t_row = jnp.minimum(t_row, past + tq - 1)

    acc_c, p_c = _compressed_branch(qrows, kc_ref[...], vc_ref[...], slope, t_row, nc)
    sel_rows = _select_blocks_shared(p_c, nq, nc, ns, ns_pad, past // SEL_BLOCK, imp_sc)

    def new_tile(k_ref, v_ref, sel):
        pos = past + lax.broadcasted_iota(jnp.int32, (1, nn), 1)
        d = t_row - pos
        ok = d >= 0
        if sel:
            ok = ok & _expand_sel(sel_rows, ns_pad, past, nn)
        s = _dot_nt(qrows, k_ref[...]) - slope * d.astype(F32)
        return s, ok, v_ref[...]

    tiles = []
    for p in range(n_pages):
        pos = p * page + lax.broadcasted_iota(jnp.int32, (1, page), 1)
        d = t_row - pos
        ok = (d >= 0) & _expand_sel(sel_rows, ns_pad, p * page, page)
        s = _dot_nt(qrows, kpages[p][...].astype(BF16)) - slope * d.astype(F32)
        tiles.append((s, ok, vpages[p][...].astype(BF16)))
    tiles.append(new_tile(kn_ref, vn_ref, True))
    acc_s, _ = _softmax_tiles(tiles)

    nwp = kwp_ref.shape[0]
    w_pos0 = past - nwp
    posw = w_pos0 + lax.broadcasted_iota(jnp.int32, (1, nwp), 1)
    d_w = t_row - posw
    ok_w = (d_w >= 0) & (d_w <= WINDOW)
    s_w = _dot_nt(qrows, kwp_ref[...].astype(BF16)) - slope * d_w.astype(F32)
    sn, okn, vn = new_tile(kwn_ref, vwn_ref, False)
    dn = t_row - (past + lax.broadcasted_iota(jnp.int32, (1, nn), 1))
    acc_w, _ = _softmax_tiles([(s_w, ok_w, vwp_ref[...].astype(BF16)), (sn, okn & (dn <= WINDOW), vn)])

    gates = jax.nn.sigmoid(gate_ref[...])
    o_ref[...] = _combine(gates, (acc_c, acc_s, acc_w), nq)


def _attn_sample(z, kvb, kc, vc, cache3, win3, page_table, tq):
    b, nq, _ = z.shape
    nqd = N_HEADS * HEAD_DIM
    n_pages = page_table.shape[1]
    page = cache3.shape[1]
    past = n_pages * page
    nn = kvb.shape[1]
    nc = kc.shape[1]
    ns = -(-(past + tq) // SEL_BLOCK)
    ns_pad = -(-ns // 128) * 128
    nwp = win3.shape[1]
    assert past % SEL_BLOCK == 0 and nn <= SEL_BLOCK and nwp == WINDOW
    assert (past + tq - 1) // SEL_BLOCK == past // SEL_BLOCK
    kp = [pl.BlockSpec((None, page, KV_W), functools.partial(lambda p, i, pt: (pt[i, p], 0, 2), p))
          for p in range(n_pages)]
    vp = [pl.BlockSpec((None, page, KV_W), functools.partial(lambda p, i, pt: (pt[i, p], 0, 3), p))
          for p in range(n_pages)]
    newspec = [pl.BlockSpec((None, nn, KV_W), functools.partial(lambda c, i, pt: (i, 0, c), c))
               for c in (2, 3, 4, 5)]
    grid_spec = pltpu.PrefetchScalarGridSpec(
        num_scalar_prefetch=1,
        grid=(b,),
        in_specs=[
            pl.BlockSpec((None, nq, nqd), lambda i, pt: (i, 0, 0)),
            pl.BlockSpec((None, nq, 128), lambda i, pt: (i, 0, nqd // 128)),
            pl.BlockSpec((None, nc, KV_W), lambda i, pt: (i, 0, 0)),
            pl.BlockSpec((None, nc, KV_W), lambda i, pt: (i, 0, 0)),
        ] + kp + vp + newspec + [
            pl.BlockSpec((None, nwp, KV_W), lambda i, pt: (i, 0, 0)),
            pl.BlockSpec((None, nwp, KV_W), lambda i, pt: (i, 0, 1)),
        ],
        out_specs=pl.BlockSpec((None, nq, nqd), lambda i, pt: (i, 0, 0)),
        scratch_shapes=[pltpu.VMEM((ns_pad, N_KV * nq), F32)],
    )
    return pl.pallas_call(
        functools.partial(_attn_sample_body, n_pages, page, past, tq, nc, ns, ns_pad),
        grid_spec=grid_spec,
        out_shape=jax.ShapeDtypeStruct((b, nq, nqd), F32),
        compiler_params=_cparams("parallel"),
        name="nsa_attn_sample",
    )(page_table, z, z, kc, vc, *([cache3] * (2 * n_pages)), kvb, kvb, kvb, kvb, win3, win3)


def kernel(x_prompt, x_sample, cache_cs, cache_win, state_shift, state_wkv, page_table, norm_g, ffn_w_in, ffn_w_out, rw_mu, rw_w_rkv, rw_w0, rw_w1, rw_w2, rw_a0, rw_a1, rw_a2, rw_g1, rw_g2, rw_k_k, rw_k_a, rw_r_k, rw_lnx_w, rw_lnx_b, rw_w_o, rw_v0, rw_v1, rw_v2, kv_norm_g, kv_w, cmp_pos, cmp_w1, cmp_w2, b_w_in, b_w_o, final_g):
    bp, tp, d = x_prompt.shape
    bs, ts, _ = x_sample.shape
    depth = norm_g.shape[0]
    n_a = rw_mu.shape[0]
    nh = d // RWKV_HEAD
    mp, ms = bp * tp, bs * ts
    n_pool, page = cache_cs.shape[:2]
    n_pages = page_table.shape[1]
    past = n_pages * page
    nqd = N_HEADS * HEAD_DIM
    ts_pad = 8
    assert tp % SCAN_CHUNK == 0 and tp % page == 0 and tp % SEL_BLOCK == 0
    nq_p = max(1, min(64, tp, 256 // bp))
    assert nq_p == SEL_BLOCK and ts <= ts_pad
    assert (past + ts - CMP_LEN) // CMP_STRIDE * CMP_STRIDE + CMP_LEN <= past

    cache3 = cache_cs.reshape(n_pool, page, 4 * KV_W)
    win3 = cache_win.reshape(bs, cache_win.shape[1], 2 * KV_W)

    def pad_tokens(t):
        return jnp.pad(t.reshape(bs, ts, -1), ((0, 0), (0, ts_pad - ts), (0, 0)))

    h = [x_prompt.reshape(mp, d), x_sample.reshape(ms, d)]
    shifts = [[], []]
    states = [[], []]
    v_first = [None, None]
    kv = kvb = kc = vc = [None, None]
    y = [None, None]
    for l in range(depth):
        g = norm_g[l]
        w_in0, w_out0 = ffn_w_in[l, 0].astype(BF16), ffn_w_out[l, 0].astype(BF16)
        w_in1, w_out1 = ffn_w_in[l, 1].astype(BF16), ffn_w_out[l, 1].astype(BF16)
        last = l == depth - 1
        if l < n_a:
            p = dict(mu=rw_mu[l], w_rkv=rw_w_rkv[l], w0=rw_w0[l], w1=rw_w1[l], w2=rw_w2[l], a0=rw_a0[l],
                     a1=rw_a1[l], a2=rw_a2[l], g1=rw_g1[l], g2=rw_g2[l], k_k=rw_k_k[l], k_a=rw_k_a[l])
            if l > 0:
                p.update(v0=rw_v0[l - 1], v1=rw_v1[l - 1], v2=rw_v2[l - 1])
            w_o = rw_w_o[l].astype(BF16)
        else:
            w_q = _pad_cols(b_w_in[l - n_a], nqd + 128).astype(BF16)
            w_o = b_w_o[l - n_a].astype(BF16)
        for s in range(2):
            h[s], xn = _ffn(h[s], g[0], w_in0, w_out0, g_post=g[1])
            if l < n_a:
                if s == 0:
                    shifts[s].append(xn.reshape(bp, tp, d)[:, -1])
                    pre = _rwkv_pre(xn, None, v_first[s], p, seq_len=tp)
                    s0, chunk = jnp.zeros((bp, nh, RWKV_HEAD, RWKV_HEAD), F32), SCAN_CHUNK
                    seqs = [t.reshape(bp, tp, d) for t in pre[:6]]
                else:
                    xs = xn.reshape(bs, ts, d)
                    shifts[s].append(xs[:, -1])
                    xprev = jnp.concatenate([state_shift[l][:, None, :], xs[:, :-1]], axis=1).reshape(ms, d)
                    pre = _rwkv_pre(xn, xprev, v_first[s], p)
                    s0, chunk = state_wkv[l], ts_pad
                    seqs = [pad_tokens(t) for t in pre[:6]]
                if l == 0:
                    v_first[s] = pre[3]
                ymix, st = _rwkv_scan(*seqs, rw_r_k[l], rw_lnx_w[l], rw_lnx_b[l], s0, chunk)
                states[s].append(st)
                ymix = ymix.reshape(mp, d) if s == 0 else ymix[:, :ts].reshape(ms, d)
                h[s] = _matmul_res(ymix, pre[6], w_o, h[s])
            else:
                z, _ = _norm_matmul(xn, None, w_q)
                if s == 0:
                    o = _attn_prompt(z.reshape(bp, tp, nqd + 128), kvb[s].reshape(bp, tp, 6 * KV_W),
                                     kc[s], vc[s], nq_p).reshape(mp, nqd)
                else:
                    o = _attn_sample(pad_tokens(z), pad_tokens(kvb[s]), kc[s], vc[s], cache3, win3,
                                     page_table, ts)[:, :ts].reshape(ms, nqd)
                h[s] = _matmul_res(o, None, w_o, h[s])
            res = _ffn(h[s], g[2], w_in1, w_out1, g_post=final_g if last else None)
            if last:
                h[s], y[s] = res
            else:
                h[s] = res
        if l == n_a - 1:
            kv_w_b = kv_w.astype(BF16)
            kv0, kvb0 = _norm_matmul(h[0], kv_norm_g, kv_w_b)
            kv1, kvb1 = _norm_matmul(h[1], kv_norm_g, kv_w_b)
            kv, kvb = [kv0, kv1], [kvb0, kvb1]
            ident = jnp.arange(mp // page, dtype=jnp.int32).reshape(bp, tp // page)
            kc0, vc0 = _compress(kv0.reshape(mp // page, page, 6 * KV_W), ident, cmp_pos, cmp_w1, cmp_w2)
            kc1, vc1 = _compress(cache3, page_table, cmp_pos, cmp_w1, cmp_w2)
            kc, vc = [kc0, kc1], [vc0, vc1]

    kv_p = kv[0].reshape(bp, tp, 6, N_KV, HEAD_DIM)
    kv_s = kv[1].reshape(bs, ts, 6, N_KV, HEAD_DIM)
    win_all = jnp.concatenate([cache_win, kv_s[:, :, 4:]], axis=1)
    s_win = win_all[:, win_all.shape[1] - cache_win.shape[1]:]
    return (y[0].reshape(bp, tp, d), y[1].reshape(bs, ts, d), jnp.stack(shifts[0]), jnp.stack(states[0]),
            kv_p[:, :, :4], kv_p[:, tp - min(WINDOW, tp):, 4:],
            jnp.stack(shifts[1]), jnp.stack(states[1]), kv_s[:, :, :4], s_win)
```

```python
import functools

import jax
import jax.numpy as jnp
from jax import lax
from jax.experimental import pallas as pl
from jax.experimental.pallas import tpu as pltpu

F32 = jnp.float32
BF16 = jnp.bfloat16

RWKV_HEAD = 64
GN_EPS = 64e-5
N_HEADS = 16
HEAD_DIM = 64
N_KV = 4
GROUP = N_HEADS // N_KV
KV_W = N_KV * HEAD_DIM
CMP_STRIDE = 16
CMP_LEN = 2 * CMP_STRIDE
CMP_HID = 2 * HEAD_DIM
SEL_BLOCK = 64
N_SEL = 16
WINDOW = 512
RMS_EPS = 1e-6
NEG_INF = -1e30
FORCED_SCORE = 1e4
REMOVED = -3e38

SCAN_CHUNK = 64
SCAN_SUB = 16
SCAN_HEADS = 16
P_GRAM, P_SOLVE, P_OUT, P_STATE = 1, 1, 1, 3
KEY_TILE = 512
VMEM_LIMIT = 56 * 1024 * 1024


def _cparams(*sem):
    return pltpu.CompilerParams(dimension_semantics=sem, vmem_limit_bytes=VMEM_LIMIT)


def _row_tile(m, cap):
    t = cap
    while t > 8 and m % t:
        t //= 2
    assert m % t == 0, (m, cap)
    return t


def _rms(x, g):
    return x * lax.rsqrt(jnp.mean(x * x, axis=-1, keepdims=True) + RMS_EPS) * g


def _dot(a, b):
    return jnp.dot(a, b, preferred_element_type=F32)


def _dot_nt(a, b):
    return lax.dot_general(a, b, (((1,), (1,)), ((), ())), preferred_element_type=F32)


def _dot_tn(a, b):
    return lax.dot_general(a, b, (((0,), (0,)), ((), ())), preferred_element_type=F32)


def _ffn_body(nf, post, h_ref, g_ref, wg_ref, wu_ref, wo_ref, gp_ref, *rest):
    if post:
        o_ref, on_ref, xn_sc, acc_sc = rest
    else:
        o_ref, xn_sc, acc_sc = rest
    f = pl.program_id(1)

    @pl.when(f == 0)
    def _():
        xn_sc[...] = _rms(h_ref[...], g_ref[...]).astype(BF16)
        acc_sc[...] = jnp.zeros_like(acc_sc)

    xn = xn_sc[...]
    gate = _dot(xn, wg_ref[...])
    up = _dot(xn, wu_ref[...])
    hid = (gate * jax.nn.sigmoid(gate) * up).astype(BF16)
    acc_sc[...] += _dot(hid, wo_ref[...])

    @pl.when(f == nf - 1)
    def _():
        hn = h_ref[...] + 0.5 * acc_sc[...]
        o_ref[...] = hn
        if post:
            on_ref[...] = _rms(hn, gp_ref[...])


def _ffn(h, g, w_in, w_out, g_post=None):
    m, d = h.shape
    dff = w_out.shape[0]
    tm = _row_tile(m, 512)
    tf = dff // 2 if (dff // 2) % 128 == 0 else dff
    nf = dff // tf
    post = g_post is not None
    gp = g_post if post else g
    out_shape = [jax.ShapeDtypeStruct((m, d), F32)]
    out_specs = [pl.BlockSpec((tm, d), lambda i, f: (i, 0))]
    if post:
        out_shape.append(jax.ShapeDtypeStruct((m, d), F32))
        out_specs.append(pl.BlockSpec((tm, d), lambda i, f: (i, 0)))
    res = pl.pallas_call(
        functools.partial(_ffn_body, nf, post),
        grid=(m // tm, nf),
        in_specs=[
            pl.BlockSpec((tm, d), lambda i, f: (i, 0)),
            pl.BlockSpec((1, d), lambda i, f: (0, 0)),
            pl.BlockSpec((d, tf), lambda i, f: (0, f)),
            pl.BlockSpec((d, tf), lambda i, f: (0, nf + f)),
            pl.BlockSpec((tf, d), lambda i, f: (f, 0)),
            pl.BlockSpec((1, d), lambda i, f: (0, 0)),
        ],
        out_specs=out_specs,
        out_shape=out_shape,
        scratch_shapes=[pltpu.VMEM((tm, d), BF16), pltpu.VMEM((tm, d), F32)],
        compiler_params=_cparams("parallel", "arbitrary"),
        name="ffn",
    )(h, g.reshape(1, d), w_in, w_in, w_out, gp.reshape(1, d))
    return res if post else res[0]


def _norm_matmul_body(norm, x_ref, g_ref, w_ref, o_ref, ob_ref):
    x = x_ref[...]
    if norm:
        x = _rms(x, g_ref[...])
    y = _dot(x.astype(BF16), w_ref[...])
    o_ref[...] = y
    ob_ref[...] = y.astype(BF16)


def _norm_matmul(x, g, w):
    m, d = x.shape
    n = w.shape[1]
    tm = _row_tile(m, 512)
    norm = g is not None
    gg = g.reshape(1, d) if norm else jnp.ones((1, d), F32)
    return pl.pallas_call(
        functools.partial(_norm_matmul_body, norm),
        grid=(m // tm,),
        in_specs=[
            pl.BlockSpec((tm, d), lambda i: (i, 0)),
            pl.BlockSpec((1, d), lambda i: (0, 0)),
            pl.BlockSpec((d, n), lambda i: (0, 0)),
        ],
        out_specs=[pl.BlockSpec((tm, n), lambda i: (i, 0)),
                   pl.BlockSpec((tm, n), lambda i: (i, 0))],
        out_shape=[jax.ShapeDtypeStruct((m, n), F32), jax.ShapeDtypeStruct((m, n), BF16)],
        compiler_params=_cparams("parallel"),
        name="norm_matmul",
    )(x, gg, w)


def _matmul_res_body(gated, x_ref, g_ref, w_ref, h_ref, o_ref):
    x = x_ref[...]
    if gated:
        x = x * g_ref[...]
    o_ref[...] = h_ref[...] + _dot(x.astype(BF16), w_ref[...])


def _matmul_res(x, gate, w, h):
    m, k = x.shape
    n = w.shape[1]
    tm = _row_tile(m, 512)
    gated = gate is not None
    gg = gate if gated else x
    return pl.pallas_call(
        functools.partial(_matmul_res_body, gated),
        grid=(m // tm,),
        in_specs=[
            pl.BlockSpec((tm, k), lambda i: (i, 0)),
            pl.BlockSpec((tm, k), lambda i: (i, 0)),
            pl.BlockSpec((k, n), lambda i: (0, 0)),
            pl.BlockSpec((tm, n), lambda i: (i, 0)),
        ],
        out_specs=pl.BlockSpec((tm, n), lambda i: (i, 0)),
        out_shape=jax.ShapeDtypeStruct((m, n), F32),
        compiler_params=_cparams("parallel"),
        name="matmul_res",
    )(x, gg, w, h)


def _softplus(x):
    return jnp.maximum(x, 0.0) + jnp.log1p(jnp.exp(-jnp.abs(x)))


def _rwkv_pre_body(vres, shift_rows, x_ref, xp_ref, vf_ref, mu_ref, wr_ref, wk_ref, wv_ref,
                   w0_ref, w1_ref, w2_ref, a0_ref, a1_ref, a2_ref, g1_ref, g2_ref,
                   v0_ref, v1_ref, v2_ref, kk_ref, ka_ref,
                   r_o, lw_o, k_o, v_o, kkr_o, a_o, g_o):
    x = x_ref[...]
    if shift_rows:
        first = lax.broadcasted_iota(jnp.int32, x.shape, 0) == 0
        xp = jnp.where(first, xp_ref[...], pltpu.roll(x, 1, axis=0))
    else:
        xp = xp_ref[...]
    xx = xp - x

    def lerp(c):
        return (x + xx * mu_ref[c:c + 1, :]).astype(BF16)

    r = _dot(lerp(0), wr_ref[...])
    k = _dot(lerp(1), wk_ref[...])
    xv = lerp(2)
    v = _dot(xv, wv_ref[...])
    w = w0_ref[...] + _dot(jnp.tanh(_dot(lerp(3), w1_ref[...])).astype(BF16), w2_ref[...])
    w = -_softplus(-w) - 0.5
    a = jax.nn.sigmoid(a0_ref[...] + _dot(_dot(lerp(4), a1_ref[...]).astype(BF16), a2_ref[...]))
    g = _dot(jax.nn.sigmoid(_dot(lerp(5), g1_ref[...])).astype(BF16), g2_ref[...])
    if vres:
        mix = jax.nn.sigmoid(v0_ref[...] + _dot(_dot(xv, v1_ref[...]).astype(BF16), v2_ref[...]))
        v = v + (vf_ref[...] - v) * mix
    r_o[...] = r
    lw_o[...] = -jnp.exp(w)
    k_o[...] = k * (1.0 + (a - 1.0) * ka_ref[...])
    v_o[...] = v
    kkr_o[...] = k * kk_ref[...]
    a_o[...] = a
    g_o[...] = g


def _pad_cols(w, n):
    return jnp.pad(w, ((0, 0), (0, n - w.shape[1])))


def _pad_rows(w, n):
    return jnp.pad(w, ((0, n - w.shape[0]), (0, 0)))


def _lora(w_down, w_up):
    r = w_down.shape[1]
    rp = -(-r // 128) * 128
    return _pad_cols(w_down, rp).astype(BF16), _pad_rows(w_up, rp).astype(BF16)


def _rwkv_pre(xn, xprev, v_first, p, seq_len=None):
    m, d = xn.shape
    tm = _row_tile(m, 256)
    shift_rows = xprev is None
    if shift_rows:
        assert seq_len % tm == 0
        n_tiles = m // tm
        tails = xn.reshape(n_tiles, tm, d)[:, tm - 1]
        prev = jnp.concatenate([jnp.zeros((1, d), F32), tails[:-1]], axis=0)
        starts = (jnp.arange(n_tiles) * tm) % seq_len == 0
        xprev = jnp.where(starts[:, None], 0.0, prev).reshape(n_tiles, 1, d)
    vres = v_first is not None
    vf = v_first if vres else xn
    w1, w2 = _lora(p["w1"], p["w2"])
    a1, a2 = _lora(p["a1"], p["a2"])
    g1, g2 = _lora(p["g1"], p["g2"])
    if vres:
        v1, v2 = _lora(p["v1"], p["v2"])
        v0 = p["v0"].reshape(1, d)
    else:
        v1, v2, v0 = a1, a2, p["a0"].reshape(1, d)
    tok = pl.BlockSpec((tm, d), lambda i: (i, 0))

    def full(a):
        return pl.BlockSpec(a.shape, lambda i: (0,) * a.ndim)

    wr, wk, wv = (p["w_rkv"][c].astype(BF16) for c in range(3))
    consts = [p["mu"], wr, wk, wv, p["w0"].reshape(1, d), w1, w2, p["a0"].reshape(1, d), a1, a2,
              g1, g2, v0, v1, v2, p["k_k"].reshape(1, d), p["k_a"].reshape(1, d)]
    return pl.pallas_call(
        functools.partial(_rwkv_pre_body, vres, shift_rows),
        grid=(m // tm,),
        in_specs=[tok, pl.BlockSpec((None, 1, d), lambda i: (i, 0, 0)) if shift_rows else tok, tok]
        + [full(c) for c in consts],
        out_specs=[tok] * 7,
        out_shape=[jax.ShapeDtypeStruct((m, d), F32)] * 7,
        compiler_params=_cparams("parallel"),
        name="rwkv_pre",
    )(xn, xprev, vf, *consts)


def _split(x):
    hi = x.astype(BF16)
    lo = (x - hi.astype(F32)).astype(BF16)
    return hi, lo


_BATCH_DIMS = {"nn": ((2,), (1,)), "nt": ((2,), (2,)), "tn": ((1,), (1,))}


def _bmm(kind, a, b, passes):
    dn = (_BATCH_DIMS[kind], ((0,), (0,)))

    def dg(x, y):
        return lax.dot_general(x, y, dn, preferred_element_type=F32)

    if passes == 1:
        return dg(a.astype(BF16), b.astype(BF16))
    ah, al = _split(a)
    bh, bl = _split(b)
    return dg(ah, bh) + (dg(ah, bl) + dg(al, bh))


def _mm_exact_lhs(a01, b):
    b1 = b.astype(BF16)
    r1 = b - b1.astype(F32)
    b2 = r1.astype(BF16)
    b3 = (r1 - b2.astype(F32)).astype(BF16)
    return _dot(a01, b1) + (_dot(a01, b2) + _dot(a01, b3))


def _scan_body(L, n_heads, r_ref, lw_ref, k_ref, v_ref, kkr_ref, a_ref, rk_ref, lnw_ref, lnb_ref,
               s0_ref, y_ref, st_ref, s_sc):
    ci = pl.program_id(1)
    N = RWKV_HEAD
    HB = min(SCAN_HEADS, n_heads)

    @pl.when(ci == 0)
    def _():
        s_sc[...] = s0_ref[...]

    row = lax.broadcasted_iota(jnp.int32, (1, L, L), 1)
    col = lax.broadcasted_iota(jnp.int32, (1, L, L), 2)
    tri_incl = row >= col
    tri_strict = row > col
    same_sub = (row // SCAN_SUB) == (col // SCAN_SUB)
    ltri = tri_incl[0].astype(BF16)
    eye = (row == col).astype(F32)
    eye_n = (lax.broadcasted_iota(jnp.int32, (1, N, N), 1)
             == lax.broadcasted_iota(jnp.int32, (1, N, N), 2)).astype(F32)

    lw = lw_ref[...]
    c = _mm_exact_lhs(ltri, lw)
    ec = jnp.exp(c)
    eci = jnp.exp(-c)
    ecp = jnp.exp(c - lw)
    r = r_ref[...]
    k = k_ref[...]
    kt_all = k * eci
    rt_all = r * ec
    be_all = a_ref[...] * eci
    rkr_all = r * k * rk_ref[...]

    for g in range(n_heads // HB):
        def heads(x):
            return jnp.stack([x[:, (g * HB + i) * N:(g * HB + i + 1) * N] for i in range(HB)])

        kkr = heads(kkr_ref[...])
        kk = kkr * lax.rsqrt(jnp.maximum(jnp.sum(kkr * kkr, axis=-1, keepdims=True), 1e-24))
        at = -kk * heads(ecp)
        bt = kk * heads(be_all)
        kt = heads(kt_all)
        rt = heads(rt_all)
        v = heads(v_ref[...])
        ab = jnp.concatenate([at, rt], axis=1)
        gram_b = _bmm("nt", ab, bt, P_GRAM)
        gram_k = _bmm("nt", ab, kt, P_GRAM)
        nab = jnp.where(tri_strict, gram_b[:, :L], 0.0)
        aak = jnp.where(tri_strict, gram_k[:, :L], 0.0)
        gb = jnp.where(tri_incl, gram_b[:, L:], 0.0)
        gk = jnp.where(tri_incl, gram_k[:, L:], 0.0)
        nd = jnp.where(same_sub, nab, 0.0)
        noff = nab - nd
        nd2 = _bmm("nn", nd, nd, P_SOLVE)
        nd4 = _bmm("nn", nd2, nd2, P_SOLVE)
        nd8 = _bmm("nn", nd4, nd4, P_SOLVE)
        td = eye + nd
        td = td + _bmm("nn", td, nd2, P_SOLVE)
        td = td + _bmm("nn", td, nd4, P_SOLVE)
        td = td + _bmm("nn", td, nd8, P_SOLVE)
        rhs = jnp.concatenate([at, _bmm("nn", aak, v, P_SOLVE)], axis=2)
        wz = _bmm("nn", td, rhs, P_SOLVE)
        if L > SCAN_SUB:
            assert L // SCAN_SUB <= 4
            mo = _bmm("nn", td, noff, P_SOLVE)
            wz = wz + _bmm("nn", mo, wz, P_SOLVE)
            wz = wz + _bmm("nn", _bmm("nn", mo, mo, P_SOLVE), wz, P_SOLVE)
        gwz = _bmm("nn", gb, wz, P_OUT)
        rp = rt + gwz[:, :, :N]
        y0 = gwz[:, :, N:] + _bmm("nn", gk, v, P_OUT)
        ecl = heads(ec[L - 1:L, :])
        wzb = _bmm("tn", wz, bt, P_STATE)
        pm = (eye_n + wzb[:, :N]) * ecl
        qm = (wzb[:, N:] + _bmm("tn", v, kt, P_STATE)) * ecl
        s_prev = s_sc[g * HB:(g + 1) * HB]
        y = _bmm("nt", rp, s_prev, P_OUT) + y0
        s_sc[g * HB:(g + 1) * HB] = _bmm("nn", s_prev, pm, P_STATE) + qm
        mean = jnp.mean(y, axis=-1, keepdims=True)
        yc = y - mean
        var = jnp.mean(yc * yc, axis=-1, keepdims=True)
        yn = yc * lax.rsqrt(var + GN_EPS) * heads(lnw_ref[...]) + heads(lnb_ref[...])
        out = yn + jnp.sum(heads(rkr_all), axis=-1, keepdims=True) * v
        y_ref[:, g * HB * N:(g + 1) * HB * N] = jnp.concatenate([out[i] for i in range(HB)], axis=1)

    @pl.when(ci == pl.num_programs(1) - 1)
    def _():
        st_ref[...] = s_sc[...]


def _rwkv_scan(r, lw, k, v, kkr, a, r_k, lnx_w, lnx_b, s0, L):
    b, t, d = r.shape
    nh = d // RWKV_HEAD
    tok = pl.BlockSpec((None, L, d), lambda i, c: (i, c, 0))
    vec = pl.BlockSpec((1, d), lambda i, c: (0, 0))
    st = pl.BlockSpec((None, nh, RWKV_HEAD, RWKV_HEAD), lambda i, c: (i, 0, 0, 0))
    return pl.pallas_call(
        functools.partial(_scan_body, L, nh),
        grid=(b, t // L),
        in_specs=[tok] * 6 + [vec] * 3 + [st],
        out_specs=[tok, st],
        out_shape=[jax.ShapeDtypeStruct((b, t, d), F32),
                   jax.ShapeDtypeStruct((b, nh, RWKV_HEAD, RWKV_HEAD), F32)],
        scratch_shapes=[pltpu.VMEM((nh, RWKV_HEAD, RWKV_HEAD), F32)],
        compiler_params=_cparams("parallel", "arbitrary"),
        name="rwkv_scan",
    )(r, lw, k, v, kkr, a, r_k.reshape(1, d), lnx_w.reshape(1, d), lnx_b.reshape(1, d), s0)


def _compress_body(n_pages, nc, pt_ref, *refs):
    pages = refs[:n_pages]
    pos_ref, w1p_ref, w1f_ref, w2_ref, kc_ref, vc_ref, ch_sc = refs[n_pages:]
    rows = pages[0].shape[0]
    per_page = rows // CMP_STRIDE
    lane_blocks = 2 * KV_W // 128
    dst = lax.broadcasted_iota(jnp.int32, (rows, rows), 0)
    src = lax.broadcasted_iota(jnp.int32, (rows, rows), 1)
    perm = (src == (dst % per_page) * CMP_STRIDE + dst // per_page).astype(BF16)
    for p in range(n_pages):
        y = _dot(perm, pages[p][...].astype(BF16))
        for lb in range(lane_blocks):
            for rr in range(CMP_STRIDE):
                ch_sc[lb, p * per_page:(p + 1) * per_page, rr * 128:(rr + 1) * 128] = (
                    y[rr * per_page:(rr + 1) * per_page, lb * 128:(lb + 1) * 128])

    last = lax.broadcasted_iota(jnp.int32, (nc, 1), 0) == nc - 1
    for t, o_ref in enumerate((kc_ref, vc_ref)):
        posb = _dot(jnp.broadcast_to(pos_ref[t], (8, pos_ref.shape[2])).astype(BF16), w1f_ref[t])[0:1]
        for pair in range(N_KV // 2):
            ch = ch_sc[t * (N_KV // 2) + pair].astype(BF16)
            for j in range(2):
                hd = 2 * pair + j
                hh = _dot(ch, w1p_ref[t, j])
                lo = hh[:, :CMP_HID]
                hi = hh[:, CMP_HID:]
                hi = jnp.where(last, 0.0, jnp.concatenate([hi[1:], hi[:1]], axis=0))
                hid = jax.nn.gelu(lo + hi + posb)
                o_ref[:, hd * HEAD_DIM:(hd + 1) * HEAD_DIM] = _dot(hid.astype(BF16), w2_ref[t]).astype(BF16)


def _compress(src, page_table, pos, w1, w2):
    b, n_pages = page_table.shape
    rows = src.shape[1]
    nc = n_pages * rows // CMP_STRIDE
    half = CMP_STRIDE * HEAD_DIM
    w1c = jnp.concatenate([w1[:, :half], w1[:, half:]], axis=2).reshape(2, CMP_STRIDE, HEAD_DIM, 2 * CMP_HID)
    zeros = jnp.zeros_like(w1c)
    w1p = jnp.stack([jnp.concatenate([w1c, zeros], axis=2), jnp.concatenate([zeros, w1c], axis=2)], axis=1)
    w1p = w1p.reshape(2, 2, CMP_STRIDE * 128, 2 * CMP_HID).astype(BF16)
    consts = [pos.reshape(2, 1, 2 * half), w1p, w1.astype(BF16), w2.astype(BF16)]
    page_specs = [pl.BlockSpec((None, rows, 2 * KV_W), functools.partial(lambda p, i, pt: (pt[i, p], 0, 0), p))
                  for p in range(n_pages)]
    out = pl.BlockSpec((None, nc, KV_W), lambda i, pt: (i, 0, 0))
    grid_spec = pltpu.PrefetchScalarGridSpec(
        num_scalar_prefetch=1,
        grid=(b,),
        in_specs=page_specs + [pl.BlockSpec(c.shape, functools.partial(lambda n, i, pt: (0,) * n, c.ndim))
                               for c in consts],
        out_specs=[out, out],
        scratch_shapes=[pltpu.VMEM((2 * KV_W // 128, nc, CMP_STRIDE * 128), F32)],
    )
    return pl.pallas_call(
        functools.partial(_compress_body, n_pages, nc),
        grid_spec=grid_spec,
        out_shape=[jax.ShapeDtypeStruct((b, nc, KV_W), BF16)] * 2,
        compiler_params=_cparams("parallel"),
        name="nsa_compress",
    )(page_table, *([src] * n_pages), *consts)


def _query_rows(q, nq):
    pieces = []
    for kh in range(N_KV):
        for g in range(GROUP):
            hd = kh * GROUP + g
            x = q[:, hd * HEAD_DIM:(hd + 1) * HEAD_DIM] * (HEAD_DIM ** -0.5)
            parts = []
            if kh:
                parts.append(jnp.zeros((nq, kh * HEAD_DIM), F32))
            parts.append(x)
            if kh < N_KV - 1:
                parts.append(jnp.zeros((nq, (N_KV - 1 - kh) * HEAD_DIM), F32))
            pieces.append(jnp.concatenate(parts, axis=1))
    return jnp.concatenate(pieces, axis=0).astype(BF16)


def _row_meta(nq, t0):
    R = N_HEADS * nq
    rho = lax.broadcasted_iota(jnp.int32, (R, 1), 0)
    head = rho // nq
    slope = jnp.exp2(-8.0 * (head + 1).astype(F32) / N_HEADS)
    return slope, t0 + rho % nq


def _softmax_tiles(tiles):
    m = None
    for s, ok, _ in tiles:
        mt = jnp.max(jnp.where(ok, s, NEG_INF), axis=-1, keepdims=True)
        m = mt if m is None else jnp.maximum(m, mt)
    l = 0.0
    es = []
    for s, ok, _ in tiles:
        e = jnp.exp(jnp.where(ok, s, NEG_INF) - m)
        es.append(e)
        l = l + jnp.sum(e, axis=-1, keepdims=True)
    inv = 1.0 / l
    acc = 0.0
    ps = []
    for e, (s, ok, vv) in zip(es, tiles):
        p = jnp.where(ok, e * inv, 0.0)
        ps.append(p)
        if vv is not None:
            acc = acc + _dot(p.astype(BF16), vv)
    return acc, ps


def _select_blocks_shared(p_c, nq, nc, ns, ns_pad, cur, imp_sc):
    blk = lax.broadcasted_iota(jnp.int32, (ns_pad, nc), 0)
    cstart = lax.broadcasted_iota(jnp.int32, (ns_pad, nc), 1) * CMP_STRIDE
    overlap_t = ((cstart < (blk + 1) * SEL_BLOCK) & (cstart + CMP_LEN > blk * SEL_BLOCK)).astype(BF16)
    cols = []
    for kh in range(N_KV):
        acc = 0.0
        for g in range(GROUP):
            r0 = (kh * GROUP + g) * nq
            acc = acc + _dot_nt(overlap_t, p_c[r0:r0 + nq].astype(BF16))
        cols.append(acc)
    imp = jnp.concatenate(cols, axis=1)
    b = lax.broadcasted_iota(jnp.int32, (ns_pad, 1), 0)
    forced = (b == 0) | (b == cur) | (b == cur - 1)
    imp = jnp.where(b <= cur, jnp.where(forced, FORCED_SCORE, imp), REMOVED)
    imp_sc[...] = imp

    def count(s2, cnt):
        other = imp_sc[pl.ds(s2, 1), :]
        ahead = (other > imp) | ((other == imp) & (b > s2))
        return cnt + jnp.where(ahead, 1.0, 0.0)

    cnt = lax.fori_loop(0, cur + 1, count, jnp.zeros(imp.shape, F32))
    sel_t = jnp.where((cnt < min(N_SEL, ns)) & (b <= cur), 1.0, 0.0).astype(BF16)
    eye = (lax.broadcasted_iota(jnp.int32, (ns_pad, ns_pad), 0)
           == lax.broadcasted_iota(jnp.int32, (ns_pad, ns_pad), 1)).astype(BF16)
    rows = []
    for kh in range(N_KV):
        rows += [_dot_tn(sel_t[:, kh * nq:(kh + 1) * nq], eye)] * GROUP
    return jnp.concatenate(rows, axis=0)


def _expand_sel(sel_rows, ns_pad, key0, n_keys):
    blk = lax.broadcasted_iota(jnp.int32, (ns_pad, n_keys), 0)
    kb = (key0 + lax.broadcasted_iota(jnp.int32, (ns_pad, n_keys), 1)) // SEL_BLOCK
    return _dot(sel_rows.astype(BF16), (blk == kb).astype(BF16)) > 0.5


def _compressed_branch(qrows, kc, vc, slope, t_row, nc):
    pos_end = lax.broadcasted_iota(jnp.int32, (1, nc), 1) * CMP_STRIDE + (CMP_LEN - 1)
    d_c = t_row - pos_end
    ok_c = d_c >= 0
    s_c = _dot_nt(qrows, kc) - slope * d_c.astype(F32)
    acc, ps = _softmax_tiles([(s_c, ok_c, vc)])
    return acc, ps[0]


def _combine(gates, accs, nq):
    outs = []
    for kh in range(N_KV):
        for g in range(GROUP):
            hd = kh * GROUP + g
            r0 = hd * nq
            o = 0.0
            for c in range(3):
                gcol = gates[:, 3 * hd + c:3 * hd + c + 1]
                o = o + gcol * accs[c][r0:r0 + nq, kh * HEAD_DIM:(kh + 1) * HEAD_DIM]
            outs.append(o)
    return jnp.concatenate(outs, axis=1)


MASK_BIG = 1e30


def _position_columns(pos):
    c = jnp.arange(SEL_BLOCK, dtype=jnp.int32)[None, :]
    pair = jnp.where(c % 2 == 0, pos // SEL_BLOCK * SEL_BLOCK, pos % SEL_BLOCK).astype(F32)
    return jnp.where(c < 6, pair, 0.0)


def _selected_features(t):
    p = jnp.arange(t, dtype=jnp.int32)[:, None]
    c = jnp.arange(SEL_BLOCK, dtype=jnp.int32)[None, :]
    f_sel = jnp.where(p // SEL_BLOCK == c, MASK_BIG, 0.0)
    return jnp.concatenate([f_sel, _position_columns(p), jnp.zeros_like(f_sel)], axis=1).astype(BF16)


def _window_features(nq):
    u = jnp.arange(WINDOW + nq, dtype=jnp.int32)[:, None]
    c = jnp.arange(SEL_BLOCK, dtype=jnp.int32)[None, :]
    f_lo = jnp.where(u < c, MASK_BIG, 0.0)
    f_hi = jnp.where(u > c + WINDOW, MASK_BIG, 0.0)
    return jnp.concatenate([jnp.zeros_like(f_lo), f_lo, f_hi, _position_columns(u)], axis=1).astype(BF16)


def _attn_prompt_body(T, nc, ns, q_ref, gate_ref, kc_ref, vc_ref, ksa_ref, vs_ref, kwa_ref, vw_ref, fw_ref,
                      o_ref, m_sc, l_sc, acc_sc, s_sc, imp_sc):
    nq = q_ref.shape[0]
    R = N_HEADS * nq
    i = pl.program_id(1)
    t0 = pl.multiple_of(i * nq, nq)
    qrows = _query_rows(q_ref[...], nq)
    slope, t_row = _row_meta(nq, t0)

    acc_c, p_c = _compressed_branch(qrows, kc_ref[...], vc_ref[...], slope, t_row, nc)
    sel_rows = _select_blocks_shared(p_c, nq, nc, ns, SEL_BLOCK, i, imp_sc)

    lane = lax.broadcasted_iota(jnp.int32, (R, SEL_BLOCK), 1)
    q_loc = lax.broadcasted_iota(jnp.int32, (R, 1), 0) % nq
    s1 = slope.astype(BF16).astype(F32)
    s2 = (slope - s1).astype(BF16).astype(F32)
    s3 = slope - s1 - s2
    f_slope = jnp.where(lane < 2, s1, jnp.where(lane < 4, s2, jnp.where(lane < 6, s3, 0.0)))
    neg_hot = jnp.where(lane == q_loc, -1.0, 0.0)
    zero = jnp.zeros((R, SEL_BLOCK), F32)
    kt = min(KEY_TILE, T)
    d0 = pl.multiple_of(jnp.maximum(t0 - (kt - nq), 0), nq)
    upto_own = jnp.where(lane <= i, sel_rows, 0.0) - 1.0
    earlier = jnp.where(lane < d0 // SEL_BLOCK, sel_rows, 0.0) - 1.0

    q_in = q_ref[...]
    q_own = jnp.concatenate([q_in[:, hd * HEAD_DIM:(hd + 1) * HEAD_DIM] for hd in range(N_HEADS)],
                            axis=0) * (HEAD_DIM ** -0.5)

    def slc_rows(sel_m1):
        return jnp.concatenate([q_own, sel_m1, f_slope, zero], axis=1).astype(BF16)

    rows_kv = GROUP * nq

    def slc_scores(qa, k0):
        return jnp.concatenate(
            [_dot_nt(qa[kh * rows_kv:(kh + 1) * rows_kv], ksa_ref[kh, pl.ds(k0, kt), :]) for kh in range(N_KV)],
            axis=0)

    qa_main = slc_rows(earlier)

    pos_d = d0 + lax.broadcasted_iota(jnp.int32, (1, kt), 1)
    s_d = jnp.where(pos_d > t_row, NEG_INF, slc_scores(slc_rows(upto_own), d0))
    m0 = jnp.max(s_d, axis=-1, keepdims=True)
    p0 = jnp.exp(s_d - m0)
    m_sc[...] = m0
    l_sc[...] = jnp.sum(p0, axis=-1, keepdims=True)
    acc_sc[...] = _dot(p0.astype(BF16), vs_ref[pl.ds(d0, kt), :])

    def scores(j):
        return slc_scores(qa_main, pl.multiple_of(jnp.minimum(j, T // kt - 1) * kt, kt))

    s_sc[0] = scores(0)

    def tile(j, carry):
        slot = j % 2
        s = s_sc[slot]
        s_sc[1 - slot] = scores(j + 1)
        k0 = pl.multiple_of(j * kt, kt)
        m_old = m_sc[...]
        m_new = jnp.maximum(m_old, jnp.max(s, axis=-1, keepdims=True))
        p = jnp.exp(s - m_new)
        scale = jnp.exp(m_old - m_new)
        l_sc[...] = scale * l_sc[...] + jnp.sum(p, axis=-1, keepdims=True)
        acc_sc[...] = scale * acc_sc[...] + _dot(p.astype(BF16), vs_ref[pl.ds(k0, kt), :])
        m_sc[...] = m_new
        return carry

    lax.fori_loop(0, (d0 + kt - 1) // kt, tile, 0)
    acc_s = acc_sc[...] * (1.0 / l_sc[...])

    nw = min(WINDOW + nq, T)

    def win_scores(qa, w0, extra):
        return jnp.concatenate(
            [_dot_nt(qa[kh * rows_kv:(kh + 1) * rows_kv], kwa_ref[kh, pl.ds(w0, nw), :] + extra)
             for kh in range(N_KV)], axis=0)

    def window_masked(_):
        w0 = pl.multiple_of(jnp.clip(t0 - WINDOW, 0, T - nw), nq)
        posw = w0 + lax.broadcasted_iota(jnp.int32, (1, nw), 1)
        d_w = t_row - posw
        ok_w = (d_w >= 0) & (d_w <= WINDOW)
        qa = jnp.concatenate([q_own, zero, zero, zero], axis=1).astype(BF16)
        s_w = win_scores(qa, w0, jnp.zeros((), BF16)) - slope * d_w.astype(F32)
        return _softmax_tiles([(s_w, ok_w, vw_ref[pl.ds(w0, nw), :])])[0]

    def window_full(_):
        w0 = pl.multiple_of(t0 - WINDOW, nq)
        qa = jnp.concatenate([q_own, neg_hot, neg_hot, f_slope], axis=1).astype(BF16)
        s_w = win_scores(qa, w0, fw_ref[...])
        e = jnp.exp(s_w - jnp.max(s_w, axis=-1, keepdims=True))
        acc = _dot(e.astype(BF16), vw_ref[pl.ds(w0, nw), :])
        return acc * (1.0 / jnp.sum(e, axis=-1, keepdims=True))

    if T >= 2 * WINDOW:
        acc_w = lax.cond(i >= WINDOW // nq, window_full, window_masked, 0)
    else:
        acc_w = window_masked(0)

    gates = jax.nn.sigmoid(gate_ref[...])
    o_ref[...] = _combine(gates, (acc_c, acc_s, acc_w), nq)


def _attn_prompt(z, kvb, kc, vc, nq):
    b, t, _ = z.shape
    nqd = N_HEADS * HEAD_DIM
    nc = kc.shape[1]
    ns = t // SEL_BLOCK
    R = N_HEADS * nq
    assert nq == SEL_BLOCK and ns <= SEL_BLOCK and KV_W == 4 * SEL_BLOCK
    once = pl.Buffered(1)
    kvspec = [pl.BlockSpec((None, t, KV_W), functools.partial(lambda c, i, j: (i, 0, c), c), pipeline_mode=once)
              for c in (3, 5)]
    k_slc = kvb[:, :, 2 * KV_W:3 * KV_W].reshape(b, t, N_KV, HEAD_DIM).transpose(0, 2, 1, 3)
    extra = jnp.broadcast_to(_selected_features(t), (b, N_KV, t, 3 * SEL_BLOCK))
    ksa = jnp.concatenate([k_slc, extra], axis=3)
    k_win = kvb[:, :, 4 * KV_W:5 * KV_W].reshape(b, t, N_KV, HEAD_DIM).transpose(0, 2, 1, 3)
    kwa = jnp.pad(k_win, ((0, 0), (0, 0), (0, 0), (0, 3 * SEL_BLOCK)))
    head_spec = pl.BlockSpec((None, N_KV, t, KV_W), lambda i, j: (i, 0, 0, 0), pipeline_mode=once)
    fwin = _window_features(nq)
    return pl.pallas_call(
        functools.partial(_attn_prompt_body, t, nc, ns),
        grid=(b, t // nq),
        in_specs=[
            pl.BlockSpec((None, nq, nqd), lambda i, j: (i, j, 0)),
            pl.BlockSpec((None, nq, 128), lambda i, j: (i, j, nqd // 128)),
            pl.BlockSpec((None, nc, KV_W), lambda i, j: (i, 0, 0)),
            pl.BlockSpec((None, nc, KV_W), lambda i, j: (i, 0, 0)),
            head_spec, kvspec[0], head_spec, kvspec[1],
            pl.BlockSpec(fwin.shape, lambda i, j: (0, 0), pipeline_mode=once),
        ],
        out_specs=pl.BlockSpec((None, nq, nqd), lambda i, j: (i, j, 0)),
        out_shape=jax.ShapeDtypeStruct((b, t, nqd), F32),
        scratch_shapes=[pltpu.VMEM((R, 1), F32), pltpu.VMEM((R, 1), F32), pltpu.VMEM((R, KV_W), F32),
                        pltpu.VMEM((2, R, min(KEY_TILE, t)), F32), pltpu.VMEM((SEL_BLOCK, N_KV * nq), F32)],
        compiler_params=_cparams("parallel", "arbitrary"),
        name="nsa_attn_prompt",
    )(z, z, kc, vc, ksa, kvb, kwa, kvb, fwin)


def _attn_sample_body(n_pages, page, past, tq, nc, ns, ns_pad, pt_ref, q_ref, gate_ref, kc_ref, vc_ref,
                      *rest):
    kpages = rest[:n_pages]
    vpages = rest[n_pages:2 * n_pages]
    kn_ref, vn_ref, kwn_ref, vwn_ref, kwp_ref, vwp_ref, o_ref, imp_sc = rest[2 * n_pages:]
    nq = q_ref.shape[0]
    nn = kn_ref.shape[0]
    qrows = _query_rows(q_ref[...], nq)
    slope, t_row = _row_meta(nq, past)
    t_row = jnp.minimum(t_row, past + tq - 1)

    acc_c, p_c = _compressed_branch(qrows, kc_ref[...], vc_ref[...], slope, t_row, nc)
    sel_rows = _select_blocks_shared(p_c, nq, nc, ns, ns_pad, past // SEL_BLOCK, imp_sc)

    def new_tile(k_ref, v_ref, sel):
        pos = past + lax.broadcasted_iota(jnp.int32, (1, nn), 1)
        d = t_row - pos
        ok = d >= 0
        if sel:
            ok = ok & _expand_sel(sel_rows, ns_pad, past, nn)
        s = _dot_nt(qrows, k_ref[...]) - slope * d.astype(F32)
        return s, ok, v_ref[...]

    tiles = []
    for p in range(n_pages):
        pos = p * page + lax.broadcasted_iota(jnp.int32, (1, page), 1)
        d = t_row - pos
        ok = (d >= 0) & _expand_sel(sel_rows, ns_pad, p * page, page)
        s = _dot_nt(qrows, kpages[p][...].astype(BF16)) - slope * d.astype(F32)
        tiles.append((s, ok, vpages[p][...].astype(BF16)))
    tiles.append(new_tile(kn_ref, vn_ref, True))
    acc_s, _ = _softmax_tiles(tiles)

    nwp = kwp_ref.shape[0]
    w_pos0 = past - nwp
    posw = w_pos0 + lax.broadcasted_iota(jnp.int32, (1, nwp), 1)
    d_w = t_row - posw
    ok_w = (d_w >= 0) & (d_w <= WINDOW)
    s_w = _dot_nt(qrows, kwp_ref[...].astype(BF16)) - slope * d_w.astype(F32)
    sn, okn, vn = new_tile(kwn_ref, vwn_ref, False)
    dn = t_row - (past + lax.broadcasted_iota(jnp.int32, (1, nn), 1))
    acc_w, _ = _softmax_tiles([(s_w, ok_w, vwp_ref[...].astype(BF16)), (sn, okn & (dn <= WINDOW), vn)])

    gates = jax.nn.sigmoid(gate_ref[...])
    o_ref[...] = _combine(gates, (acc_c, acc_s, acc_w), nq)


def _attn_sample(z, kvb, kc, vc, cache3, win3, page_table, tq):
    b, nq, _ = z.shape
    nqd = N_HEADS * HEAD_DIM
    n_pages = page_table.shape[1]
    page = cache3.shape[1]
    past = n_pages * page
    nn = kvb.shape[1]
    nc = kc.shape[1]
    ns = -(-(past + tq) // SEL_BLOCK)
    ns_pad = -(-ns // 128) * 128
    nwp = win3.shape[1]
    assert past % SEL_BLOCK == 0 and nn <= SEL_BLOCK and nwp == WINDOW
    assert (past + tq - 1) // SEL_BLOCK == past // SEL_BLOCK
    kp = [pl.BlockSpec((None, page, KV_W), functools.partial(lambda p, i, pt: (pt[i, p], 0, 2), p))
          for p in range(n_pages)]
    vp = [pl.BlockSpec((None, page, KV_W), functools.partial(lambda p, i, pt: (pt[i, p], 0, 3), p))
          for p in range(n_pages)]
    newspec = [pl.BlockSpec((None, nn, KV_W), functools.partial(lambda c, i, pt: (i, 0, c), c))
               for c in (2, 3, 4, 5)]
    grid_spec = pltpu.PrefetchScalarGridSpec(
        num_scalar_prefetch=1,
        grid=(b,),
        in_specs=[
            pl.BlockSpec((None, nq, nqd), lambda i, pt: (i, 0, 0)),
            pl.BlockSpec((None, nq, 128), lambda i, pt: (i, 0, nqd // 128)),
            pl.BlockSpec((None, nc, KV_W), lambda i, pt: (i, 0, 0)),
            pl.BlockSpec((None, nc, KV_W), lambda i, pt: (i, 0, 0)),
        ] + kp + vp + newspec + [
            pl.BlockSpec((None, nwp, KV_W), lambda i, pt: (i, 0, 0)),
            pl.BlockSpec((None, nwp, KV_W), lambda i, pt: (i, 0, 1)),
        ],
        out_specs=pl.BlockSpec((None, nq, nqd), lambda i, pt: (i, 0, 0)),
        scratch_shapes=[pltpu.VMEM((ns_pad, N_KV * nq), F32)],
    )
    return pl.pallas_call(
        functools.partial(_attn_sample_body, n_pages, page, past, tq, nc, ns, ns_pad),
        grid_spec=grid_spec,
        out_shape=jax.ShapeDtypeStruct((b, nq, nqd), F32),
        compiler_params=_cparams("parallel"),
        name="nsa_attn_sample",
    )(page_table, z, z, kc, vc, *([cache3] * (2 * n_pages)), kvb, kvb, kvb, kvb, win3, win3)


def kernel(x_prompt, x_sample, cache_cs, cache_win, state_shift, state_wkv, page_table, norm_g, ffn_w_in, ffn_w_out, rw_mu, rw_w_rkv, rw_w0, rw_w1, rw_w2, rw_a0, rw_a1, rw_a2, rw_g1, rw_g2, rw_k_k, rw_k_a, rw_r_k, rw_lnx_w, rw_lnx_b, rw_w_o, rw_v0, rw_v1, rw_v2, kv_norm_g, kv_w, cmp_pos, cmp_w1, cmp_w2, b_w_in, b_w_o, final_g):
    bp, tp, d = x_prompt.shape
    bs, ts, _ = x_sample.shape
    depth = norm_g.shape[0]
    n_a = rw_mu.shape[0]
    nh = d // RWKV_HEAD
    mp, ms = bp * tp, bs * ts
    n_pool, page = cache_cs.shape[:2]
    n_pages = page_table.shape[1]
    past = n_pages * page
    nqd = N_HEADS * HEAD_DIM
    ts_pad = 8
    assert tp % SCAN_CHUNK == 0 and tp % page == 0 and tp % SEL_BLOCK == 0
    nq_p = max(1, min(64, tp, 256 // bp))
    assert nq_p == SEL_BLOCK and ts <= ts_pad
    assert (past + ts - CMP_LEN) // CMP_STRIDE * CMP_STRIDE + CMP_LEN <= past

    cache3 = cache_cs.reshape(n_pool, page, 4 * KV_W)
    win3 = cache_win.reshape(bs, cache_win.shape[1], 2 * KV_W)

    def pad_tokens(t):
        return jnp.pad(t.reshape(bs, ts, -1), ((0, 0), (0, ts_pad - ts), (0, 0)))

    h = [x_prompt.reshape(mp, d), x_sample.reshape(ms, d)]
    shifts = [[], []]
    states = [[], []]
    v_first = [None, None]
    kv = kvb = kc = vc = [None, None]
    y = [None, None]
    for l in range(depth):
        g = norm_g[l]
        w_in0, w_out0 = ffn_w_in[l, 0].astype(BF16), ffn_w_out[l, 0].astype(BF16)
        w_in1, w_out1 = ffn_w_in[l, 1].astype(BF16), ffn_w_out[l, 1].astype(BF16)
        last = l == depth - 1
        if l < n_a:
            p = dict(mu=rw_mu[l], w_rkv=rw_w_rkv[l], w0=rw_w0[l], w1=rw_w1[l], w2=rw_w2[l], a0=rw_a0[l],
                     a1=rw_a1[l], a2=rw_a2[l], g1=rw_g1[l], g2=rw_g2[l], k_k=rw_k_k[l], k_a=rw_k_a[l])
            if l > 0:
                p.update(v0=rw_v0[l - 1], v1=rw_v1[l - 1], v2=rw_v2[l - 1])
            w_o = rw_w_o[l].astype(BF16)
        else:
            w_q = _pad_cols(b_w_in[l - n_a], nqd + 128).astype(BF16)
            w_o = b_w_o[l - n_a].astype(BF16)
        for s in range(2):
            h[s], xn = _ffn(h[s], g[0], w_in0, w_out0, g_post=g[1])
            if l < n_a:
                if s == 0:
                    shifts[s].append(xn.reshape(bp, tp, d)[:, -1])
                    pre = _rwkv_pre(xn, None, v_first[s], p, seq_len=tp)
                    s0, chunk = jnp.zeros((bp, nh, RWKV_HEAD, RWKV_HEAD), F32), SCAN_CHUNK
                    seqs = [t.reshape(bp, tp, d) for t in pre[:6]]
                else:
                    xs = xn.reshape(bs, ts, d)
                    shifts[s].append(xs[:, -1])
                    xprev = jnp.concatenate([state_shift[l][:, None, :], xs[:, :-1]], axis=1).reshape(ms, d)
                    pre = _rwkv_pre(xn, xprev, v_first[s], p)
                    s0, chunk = state_wkv[l], ts_pad
                    seqs = [pad_tokens(t) for t in pre[:6]]
                if l == 0:
                    v_first[s] = pre[3]
                ymix, st = _rwkv_scan(*seqs, rw_r_k[l], rw_lnx_w[l], rw_lnx_b[l], s0, chunk)
                states[s].append(st)
                ymix = ymix.reshape(mp, d) if s == 0 else ymix[:, :ts].reshape(ms, d)
                h[s] = _matmul_res(ymix, pre[6], w_o, h[s])
            else:
                z, _ = _norm_matmul(xn, None, w_q)
                if s == 0:
                    o = _attn_prompt(z.reshape(bp, tp, nqd + 128), kvb[s].reshape(bp, tp, 6 * KV_W),
                                     kc[s], vc[s], nq_p).reshape(mp, nqd)
                else:
                    o = _attn_sample(pad_tokens(z), pad_tokens(kvb[s]), kc[s], vc[s], cache3, win3,
                                     page_table, ts)[:, :ts].reshape(ms, nqd)
                h[s] = _matmul_res(o, None, w_o, h[s])
            res = _ffn(h[s], g[2], w_in1, w_out1, g_post=final_g if last else None)
            if last:
                h[s], y[s] = res
            else:
                h[s] = res
        if l == n_a - 1:
            kv_w_b = kv_w.astype(BF16)
            kv0, kvb0 = _norm_matmul(h[0], kv_norm_g, kv_w_b)
            kv1, kvb1 = _norm_matmul(h[1], kv_norm_g, kv_w_b)
            kv, kvb = [kv0, kv1], [kvb0, kvb1]
            ident = jnp.arange(mp // page, dtype=jnp.int32).reshape(bp, tp // page)
            kc0, vc0 = _compress(kv0.reshape(mp // page, page, 6 * KV_W), ident, cmp_pos, cmp_w1, cmp_w2)
            kc1, vc1 = _compress(cache3, page_table, cmp_pos, cmp_w1, cmp_w2)
            kc, vc = [kc0, kc1], [vc0, vc1]

    kv_p = kv[0].reshape(bp, tp, 6, N_KV, HEAD_DIM)
    kv_s = kv[1].reshape(bs, ts, 6, N_KV, HEAD_DIM)
    win_all = jnp.concatenate([cache_win, kv_s[:, :, 4:]], axis=1)
    s_win = win_all[:, win_all.shape[1] - cache_win.shape[1]:]
    return (y[0].reshape(bp, tp, d), y[1].reshape(bs, ts, d), jnp.stack(shifts[0]), jnp.stack(states[0]),
            kv_p[:, :, :4], kv_p[:, tp - min(WINDOW, tp):, 4:],
            jnp.stack(shifts[1]), jnp.stack(states[1]), kv_s[:, :, :4], s_win)
```
